```python
import jax
import jax.numpy as jnp
from jax import lax
import numpy as np


D_MODEL = 1024
BATCH = 4
SEQ = 4096
DEPTH = 4

HEAD_DIM = 64
MIX_HEADS = 12
MIX_WIDTH = MIX_HEADS * HEAD_DIM
MEM_HEADS = 4
MEM_WIDTH = MEM_HEADS * HEAD_DIM
MEM_TOKENS = 256
DECAY_LORA = 64
ICLR_LORA = 64
VALUE_LORA = 32
GATE_LORA = 128
A_COLS_FIRST = 3 * MIX_WIDTH + DECAY_LORA + ICLR_LORA + GATE_LORA
A_COLS_REST = A_COLS_FIRST + VALUE_LORA
N_A_LAYERS = DEPTH // 2
N_B_LAYERS = DEPTH - N_A_LAYERS
N_DENSE = (DEPTH + 1) // 2
N_MOE = DEPTH // 2
D_FF = 2816
N_EXPERTS = 8
TOP_K = 2
Q_BLOCK = 128
LN_EPS = 1e-5
GN_EPS = 64e-5
DEEPNORM_ALPHA = (2 * DEPTH) ** 0.25
DEEPNORM_BETA = (8 * DEPTH) ** -0.25

kernel_name = 'hybrid_rwkv7_stickbreak_yoco_moe'


def _f32(t):
    return t.astype(jnp.float32)


def layer_norm(x, g, b):
    xf = _f32(x)
    mu = jnp.mean(xf, -1, keepdims=True)
    var = jnp.mean(jnp.square(xf - mu), -1, keepdims=True)
    y = (xf - mu) * lax.rsqrt(var + LN_EPS)
    return (y * _f32(g) + _f32(b)).astype(x.dtype)


def split_heads(t):
    return t.reshape(t.shape[0], t.shape[1], -1, HEAD_DIM)


def wkv7_scan(r, w, k, v, a, b):
    bsz, _, h, n = r.shape

    def step(S, inp):
        r_t, w_t, k_t, v_t, a_t, b_t = inp
        sa = jnp.einsum('bhij,bhj->bhi', S, a_t)
        S = S * w_t[:, :, None, :] + sa[..., None] * b_t[:, :, None, :] + v_t[..., None] * k_t[:, :, None, :]
        return S, jnp.einsum('bhij,bhj->bhi', S, r_t)

    xs = tuple(jnp.moveaxis(t, 1, 0) for t in (r, w, k, v, a, b))
    s0 = jnp.zeros((bsz, h, n, n), jnp.float32)
    _, ys = lax.scan(step, s0, xs)
    return jnp.moveaxis(ys, 0, 1)


def rwkv7_time_mix(proj, mu, vec, w_up, a_up, g_up, r_k, v_gate, v_first):
    has_vres = v_gate is not None
    bsz, seq, _ = proj.shape
    prev = jnp.pad(proj[:, :-1], ((0, 0), (1, 0), (0, 0)))
    xs = proj + mu * (prev - proj)
    sizes = [MIX_WIDTH] * 3 + [DECAY_LORA, ICLR_LORA] + ([VALUE_LORA] if has_vres else []) + [GATE_LORA]
    parts = jnp.split(xs, np.cumsum(sizes)[:-1].tolist(), axis=-1)
    r, k, v, wd, ad = parts[:5]
    gd = parts[-1]
    w0, a0, k_k, k_a, lnx_g, lnx_b = (vec[i] for i in range(6))
    log_w = -jax.nn.softplus(-(w0 + jnp.tanh(wd) @ w_up)) - 0.5
    decay = jnp.exp(-jnp.exp(_f32(log_w)))
    a = jax.nn.sigmoid(a0 + ad @ a_up)
    if has_vres:
        v0, v_up = v_gate
        v = v + (v_first - v) * jax.nn.sigmoid(v0 + parts[5] @ v_up)
    else:
        v_first = v
    g = jax.nn.sigmoid(gd) @ g_up
    kk = _f32(split_heads(k * k_k))
    kk = kk / jnp.maximum(jnp.linalg.norm(kk, axis=-1, keepdims=True), 1e-12)
    k = k * (1 + (a - 1) * k_a)
    rh, kh, vh = _f32(split_heads(r)), _f32(split_heads(k)), _f32(split_heads(v))
    ah = _f32(split_heads(a))
    y = wkv7_scan(rh, split_heads(decay), kh, vh, -kk, kk * ah)
    mu_y = jnp.mean(y, -1, keepdims=True)
    var_y = jnp.mean(jnp.square(y - mu_y), -1, keepdims=True)
    y = ((y - mu_y) * lax.rsqrt(var_y + GN_EPS)).reshape(bsz, seq, MIX_WIDTH)
    y = y * _f32(lnx_g) + _f32(lnx_b)
    bonus = jnp.sum(rh * kh * _f32(r_k), -1, keepdims=True) * vh
    y = y + bonus.reshape(bsz, seq, MIX_WIDTH)
    return y.astype(proj.dtype) * g, v_first


def stick_breaking_attention(q, k, v):
    bsz, seq, h, n = q.shape
    scale = n ** -0.5
    outs = []
    for blk in range(seq // Q_BLOCK):
        t0 = blk * Q_BLOCK
        kv_len = t0 + Q_BLOCK
        z = _f32(jnp.einsum('bqhd,bkhd->bhqk', q[:, t0:kv_len], k[:, :kv_len])) * scale
        q_pos = t0 + jnp.arange(Q_BLOCK)
        k_pos = jnp.arange(kv_len)
        causal = k_pos[None, :] < q_pos[:, None]
        log_1m = jnp.where(causal, jax.nn.log_sigmoid(-z), 0.0)
        log_rest = lax.cumsum(log_1m, axis=3, reverse=True) - log_1m
        wts = jnp.where(causal, jnp.exp(jax.nn.log_sigmoid(z) + log_rest), 0.0)
        outs.append(jnp.einsum('bhqk,bkhd->bqhd', wts.astype(v.dtype), v[:, :kv_len]))
    return jnp.concatenate(outs, axis=1).reshape(bsz, seq, h * n)


def memory_attention(q_mem, mem, w_mem_kv):
    mk, mv = jnp.split(mem @ w_mem_kv, 2, axis=-1)
    q, mk, mv = split_heads(q_mem), split_heads(mk), split_heads(mv)
    s = _f32(jnp.einsum('bthd,bmhd->bhtm', q, mk)) * (HEAD_DIM ** -0.5)
    p = jax.nn.softmax(s, axis=-1).astype(mv.dtype)
    o = jnp.einsum('bhtm,bmhd->bthd', p, mv)
    return o.reshape(q_mem.shape)


def swiglu(x, w_gate_up, w_down):
    gate, up = jnp.split(x @ w_gate_up, 2, axis=-1)
    return (jax.nn.silu(gate) * up) @ w_down


def moe_swiglu(x, w_router, w_gate_up, w_down):
    logits = _f32(x @ w_router)
    top_logits, top_idx = lax.top_k(logits, TOP_K)
    top_w = jax.nn.softmax(top_logits, axis=-1)
    combine = jnp.sum(jax.nn.one_hot(top_idx, N_EXPERTS, dtype=jnp.float32) * top_w[..., None], axis=-2).astype(x.dtype)
    y = jnp.zeros_like(x)
    for e in range(N_EXPERTS):
        y = y + combine[..., e:e + 1] * swiglu(x, w_gate_up[e], w_down[e])
    return y


def setup_inputs(seed: int = 0) -> dict:
    key = jax.random.key(seed)
    ks = iter(jax.random.split(key, 48))

    def nrm(shape, scale):
        return jax.random.normal(next(ks), shape, jnp.float32) * scale

    def unif(shape, lo, hi):
        return jax.random.uniform(next(ks), shape, jnp.float32, minval=lo, maxval=hi)

    d_in = D_MODEL ** -0.5
    x = nrm((BATCH, SEQ, D_MODEL), 1.0)
    mem = nrm((BATCH, MEM_TOKENS, D_MODEL), 1.0)
    a_w_in_first = nrm((D_MODEL, A_COLS_FIRST + MEM_WIDTH), d_in)
    a_w_in_rest = nrm((N_A_LAYERS - 1, D_MODEL, A_COLS_REST + MEM_WIDTH), d_in)
    a_mu_first = unif((A_COLS_FIRST,), 0.0, 1.0)
    a_mu_rest = unif((N_A_LAYERS - 1, A_COLS_REST), 0.0, 1.0)
    a_vec = jnp.stack([
        unif((N_A_LAYERS, MIX_WIDTH), -6.0, 1.0),
        nrm((N_A_LAYERS, MIX_WIDTH), 0.5),
        0.85 + nrm((N_A_LAYERS, MIX_WIDTH), 0.05),
        1.0 + nrm((N_A_LAYERS, MIX_WIDTH), 0.05),
        1.0 + nrm((N_A_LAYERS, MIX_WIDTH), 0.05),
        nrm((N_A_LAYERS, MIX_WIDTH), 0.01),
    ], axis=1)
    a_w_up = nrm((N_A_LAYERS, DECAY_LORA, MIX_WIDTH), 0.5 * DECAY_LORA ** -0.5)
    a_a_up = nrm((N_A_LAYERS, ICLR_LORA, MIX_WIDTH), ICLR_LORA ** -0.5)
    a_g_up = nrm((N_A_LAYERS, GATE_LORA, MIX_WIDTH), GATE_LORA ** -0.5)
    a_v0 = nrm((N_A_LAYERS - 1, MIX_WIDTH), 0.5)
    a_v_up = nrm((N_A_LAYERS - 1, VALUE_LORA, MIX_WIDTH), VALUE_LORA ** -0.5)
    a_r_k = nrm((N_A_LAYERS, MIX_HEADS, HEAD_DIM), 0.1)
    b_w_in = nrm((N_B_LAYERS, D_MODEL, MIX_WIDTH + MEM_WIDTH), d_in)
    w_kv_shared = nrm((D_MODEL, 2 * MIX_WIDTH), d_in)
    mem_kv = nrm((DEPTH, D_MODEL, 2 * MEM_WIDTH), d_in)
    w_o = nrm((DEPTH, MIX_WIDTH + MEM_WIDTH, D_MODEL), (MIX_WIDTH + MEM_WIDTH) ** -0.5 * DEEPNORM_BETA)
    ln = jnp.stack([
        1.0 + nrm((DEPTH, D_MODEL), 0.05), nrm((DEPTH, D_MODEL), 0.01),
        1.0 + nrm((DEPTH, D_MODEL), 0.05), nrm((DEPTH, D_MODEL), 0.01),
    ], axis=1)
    ffn_gate_up = nrm((N_DENSE, D_MODEL, 2 * D_FF), d_in)
    ffn_down = nrm((N_DENSE, D_FF, D_MODEL), D_FF ** -0.5 * DEEPNORM_BETA)
    router = nrm((N_MOE, D_MODEL, N_EXPERTS), d_in)
    exp_gate_up = nrm((N_MOE, N_EXPERTS, D_MODEL, 2 * D_FF), d_in)
    exp_down = nrm((N_MOE, N_EXPERTS, D_FF, D_MODEL), D_FF ** -0.5 * DEEPNORM_BETA)
    return {'x': x, 'mem': mem, 'a_w_in_first': a_w_in_first, 'a_w_in_rest': a_w_in_rest,
            'a_mu_first': a_mu_first, 'a_mu_rest': a_mu_rest, 'a_vec': a_vec, 'a_w_up': a_w_up,
            'a_a_up': a_a_up, 'a_g_up': a_g_up, 'a_v0': a_v0, 'a_v_up': a_v_up, 'a_r_k': a_r_k,
            'b_w_in': b_w_in, 'w_kv_shared': w_kv_shared, 'mem_kv': mem_kv, 'w_o': w_o, 'ln': ln,
            'ffn_gate_up': ffn_gate_up, 'ffn_down': ffn_down, 'router': router,
            'exp_gate_up': exp_gate_up, 'exp_down': exp_down}


def reference(x, mem, a_w_in_first, a_w_in_rest, a_mu_first, a_mu_rest, a_vec, a_w_up, a_a_up, a_g_up,
              a_v0, a_v_up, a_r_k, b_w_in, w_kv_shared, mem_kv, w_o, ln, ffn_gate_up, ffn_down,
              router, exp_gate_up, exp_down):
    v_first = None
    k_sh = None
    v_sh = None
    for l in range(DEPTH):
        if l < N_A_LAYERS:
            w_in = a_w_in_first if l == 0 else a_w_in_rest[l - 1]
            mu = a_mu_first if l == 0 else a_mu_rest[l - 1]
            v_gate = None if l == 0 else (a_v0[l - 1], a_v_up[l - 1])
            proj = x @ w_in
            n_cols = proj.shape[-1] - MEM_WIDTH
            q_mem = proj[..., n_cols:]
            tok, v_first = rwkv7_time_mix(proj[..., :n_cols], mu, a_vec[l], a_w_up[l], a_a_up[l],
                                          a_g_up[l], a_r_k[l], v_gate, v_first)
        else:
            proj = x @ b_w_in[l - N_A_LAYERS]
            q, q_mem = proj[..., :MIX_WIDTH], proj[..., MIX_WIDTH:]
            tok = stick_breaking_attention(split_heads(q), k_sh, v_sh)
        mo = memory_attention(q_mem, mem, mem_kv[l])
        mix = jnp.concatenate([tok, mo], axis=-1) @ w_o[l]
        x = layer_norm(DEEPNORM_ALPHA * x + mix, ln[l, 0], ln[l, 1])
        if l % 2 == 0:
            ff = swiglu(x, ffn_gate_up[l // 2], ffn_down[l // 2])
        else:
            ff = moe_swiglu(x, router[l // 2], exp_gate_up[l // 2], exp_down[l // 2])
        x = layer_norm(DEEPNORM_ALPHA * x + ff, ln[l, 2], ln[l, 3])
        if l == N_A_LAYERS - 1:
            kv = x @ w_kv_shared
            k_sh = split_heads(kv[..., :MIX_WIDTH])
            v_sh = split_heads(kv[..., MIX_WIDTH:])
    return x
```

```python
import functools

import jax
import jax.numpy as jnp
from jax import lax
from jax.experimental import pallas as pl
from jax.experimental.pallas import tpu as pltpu

F32 = jnp.float32
BF16 = jnp.bfloat16

HEAD_DIM = 64
LANES = 128
MIX_WIDTH = 768
MEM_WIDTH = 256
DECAY_LORA = 64
ICLR_LORA = 64
VALUE_LORA = 32
GATE_LORA = 128
TOP_K = 2
LN_EPS = 1e-5
GN_EPS = 64e-5
WKV_CHUNK = 64
ATT_BLOCK = 128
VMEM_LIMIT = 56 * 1024 * 1024


def _params(*sem):
    return pltpu.CompilerParams(dimension_semantics=sem, vmem_limit_bytes=VMEM_LIMIT)


def _dot(a, b):
    return jnp.dot(a.astype(BF16), b.astype(BF16), preferred_element_type=F32)


def _dot_nt(a, b):
    return lax.dot_general(a.astype(BF16), b.astype(BF16), (((1,), (1,)), ((), ())),
                           preferred_element_type=F32)


def _dot_tn(a, b):
    return lax.dot_general(a.astype(BF16), b.astype(BF16), (((0,), (0,)), ((), ())),
                           preferred_element_type=F32)


def _split3(x):
    x1 = x.astype(BF16)
    r1 = x - x1.astype(F32)
    x2 = r1.astype(BF16)
    x3 = (r1 - x2.astype(F32)).astype(BF16)
    return x1, x2, x3


def _dot_sel(x, sel):
    x1, x2, x3 = _split3(x)
    return _dot(x1, sel) + _dot(x2, sel) + _dot(x3, sel)


def _sel_dot(sel, x):
    x1, x2, x3 = _split3(x)
    return _dot(sel, x1) + _dot(sel, x2) + _dot(sel, x3)


def _softplus(z):
    return jnp.maximum(z, 0.0) + jnp.log1p(jnp.exp(-jnp.abs(z)))


def _sigmoid(z):
    return 1.0 / (1.0 + jnp.exp(-z))


def _layer_norm(y, g, b):
    mu = jnp.mean(y, axis=-1, keepdims=True)
    d = y - mu
    var = jnp.mean(d * d, axis=-1, keepdims=True)
    return d * lax.rsqrt(var + LN_EPS) * g + b


def _iota(shape, dim):
    return lax.broadcasted_iota(jnp.int32, shape, dim)


def _mm_kernel(x_ref, w_ref, o_ref):
    o_ref[...] = _dot(x_ref[...], w_ref[...]).astype(o_ref.dtype)


def _matmul(x, w, tn, out_dtype=F32, tm=512):
    m, k = x.shape
    n = w.shape[1]
    tm = min(tm, m)
    return pl.pallas_call(
        _mm_kernel,
        out_shape=jax.ShapeDtypeStruct((m, n), out_dtype),
        grid=(n // tn, m // tm),
        in_specs=[pl.BlockSpec((tm, k), lambda j, i: (i, 0)),
                  pl.BlockSpec((k, tn), lambda j, i: (0, j))],
        out_specs=pl.BlockSpec((tm, tn), lambda j, i: (i, j)),
        compiler_params=_params("arbitrary", "arbitrary"),
        name="matmul",
    )(x, w)


def _prep_kernel(*refs, has_vres):
    if has_vres:
        (rkv_ref, lora_ref, vd_ref, vf_ref, mu_rkv_ref, mu_lora_ref, mu_vd_ref, vec_ref,
         ww_ref, wa_ref, wg_ref, wv_ref, bo_ref,
         r_o, lw_o, k_o, v_o, a_o, b_o, g_o, c_rkv, c_lora, c_vd) = refs
    else:
        (rkv_ref, lora_ref, mu_rkv_ref, mu_lora_ref, vec_ref,
         ww_ref, wa_ref, wg_ref, bo_ref,
         r_o, lw_o, k_o, v_o, a_o, b_o, g_o, c_rkv, c_lora) = refs

    t = pl.program_id(1)
    carries = (c_rkv, c_lora) + ((c_vd,) if has_vres else ())

    @pl.when(t == 0)
    def _():
        for c in carries:
            c[...] = jnp.zeros_like(c)

    def shifted(x_ref, mu_ref, carry_ref):
        cur = x_ref[...]
        rows = cur.shape[0]
        rolled = pltpu.roll(cur, 1, 0)
        prev = jnp.where(_iota(cur.shape, 0) == 0, carry_ref[0:1, :], rolled)
        carry_ref[0:1, :] = cur[rows - 1:rows, :]
        return cur + mu_ref[...] * (prev - cur)

    xs = shifted(rkv_ref, mu_rkv_ref, c_rkv)
    xl = shifted(lora_ref, mu_lora_ref, c_lora)
    r = xs[:, :MIX_WIDTH]
    k = xs[:, MIX_WIDTH:2 * MIX_WIDTH]
    v = xs[:, 2 * MIX_WIDTH:]
    wa_in = xl[:, :LANES]
    gd = xl[:, LANES:]
    w0, a0, k_k, k_a = (vec_ref[i:i + 1, :] for i in range(4))

    log_w = -_softplus(-(w0 + _dot(jnp.tanh(wa_in), ww_ref[...]))) - 0.5
    lw_o[...] = -jnp.exp(log_w)
    a_lr = _sigmoid(a0 + _dot(wa_in, wa_ref[...]))
    if has_vres:
        xv = shifted(vd_ref, mu_vd_ref, c_vd)
        v0 = vec_ref[4:5, :]
        v = v + (vf_ref[...] - v) * _sigmoid(v0 + _dot(xv, wv_ref[...]))
    g_o[...] = _dot(_sigmoid(gd), wg_ref[...])
    kk = k * k_k
    norm = jnp.sqrt(_dot_sel(kk * kk, bo_ref[...]))
    kk = kk / jnp.maximum(norm, 1e-12)
    r_o[...] = r
    k_o[...] = k * (1.0 + (a_lr - 1.0) * k_a)
    v_o[...] = v
    a_o[...] = -kk
    b_o[...] = kk * a_lr


def _rwkv_prep(proj, vd, v_first, mu_rkv, mu_lora, mu_vd, vec, ww, wa, wg, wv, bo, batch, seq):
    has_vres = vd is not None
    m = proj.shape[0]
    tt = min(256, seq)
    nt = seq // tt
    rkv_w = 3 * MIX_WIDTH
    row = lambda b, t: (b * nt + t, 0)
    const = lambda b, t: (0, 0)
    full = lambda a: pl.BlockSpec(a.shape, const)
    in_specs = [pl.BlockSpec((tt, rkv_w), row),
                pl.BlockSpec((tt, 2 * LANES), lambda b, t: (b * nt + t, rkv_w // (2 * LANES)))]
    args = [proj, proj]
    if has_vres:
        in_specs += [pl.BlockSpec((tt, LANES), row), pl.BlockSpec((tt, MIX_WIDTH), row)]
        args += [vd, v_first]
    in_specs += [full(mu_rkv), full(mu_lora)]
    args += [mu_rkv, mu_lora]
    if has_vres:
        in_specs.append(full(mu_vd))
        args.append(mu_vd)
    in_specs += [full(vec), full(ww), full(wa), full(wg)]
    args += [vec, ww, wa, wg]
    if has_vres:
        in_specs.append(full(wv))
        args.append(wv)
    in_specs.append(full(bo))
    args.append(bo)
    scratch = [pltpu.VMEM((8, rkv_w), F32), pltpu.VMEM((8, 2 * LANES), F32)]
    if has_vres:
        scratch.append(pltpu.VMEM((8, LANES), F32))
    out = jax.ShapeDtypeStruct((m, MIX_WIDTH), F32)
    return pl.pallas_call(
        functools.partial(_prep_kernel, has_vres=has_vres),
        out_shape=(out,) * 7,
        grid=(batch, nt),
        in_specs=in_specs,
        out_specs=(pl.BlockSpec((tt, MIX_WIDTH), row),) * 7,
        scratch_shapes=scratch,
        compiler_params=_params("arbitrary", "arbitrary"),
        name="rwkv_prep",
    )(*args)


def _wkv_kernel(r_ref, lw_ref, k_ref, v_ref, a_ref, b_ref, g_ref, vec_ref, o_ref, h_ref):
    c = pl.program_id(1)
    chunk = r_ref.shape[0]
    two = 2 * chunk

    @pl.when(c == 0)
    def _():
        h_ref[...] = jnp.zeros_like(h_ref)

    tri = (_iota((chunk, chunk), 1) <= _iota((chunk, chunk), 0)).astype(BF16)
    log_p_all = _sel_dot(tri, lw_ref[...])
    head0 = _iota((1, LANES), 1) < HEAD_DIM
    rr, cc = _iota((two, two), 0), _iota((two, two), 1)
    strict, incl = cc < rr, cc <= rr
    block_ones = ((rr // HEAD_DIM) == (cc // HEAD_DIM)).astype(BF16)
    lnx_g, lnx_b, r_k = (vec_ref[i:i + 1, :] for i in range(3))

    def stack(x):
        return jnp.concatenate([jnp.where(head0, x, 0.0), jnp.where(head0, 0.0, x)], axis=0)

    for p in range(r_ref.shape[1] // LANES):
        sl = slice(p * LANES, (p + 1) * LANES)
        lw, log_p = lw_ref[:, sl], log_p_all[:, sl]
        log_p_end = log_p[chunk - 1:chunk, :]
        inv_p = jnp.exp(-log_p)
        to_end = jnp.exp(log_p_end - log_p)
        r, k, v, a, b = r_ref[:, sl], k_ref[:, sl], v_ref[:, sl], a_ref[:, sl], b_ref[:, sl]
        lhs = jnp.concatenate([stack(a * jnp.exp(log_p - lw)), stack(r * jnp.exp(log_p))], axis=0)
        rhs = jnp.concatenate([stack(b * inv_p), stack(k * inv_p)], axis=0)
        gram = _dot_nt(lhs, rhs)
        a_ab = jnp.where(strict, gram[:two, :two], 0.0)
        a_ak = jnp.where(strict, gram[:two, two:], 0.0)
        a_rb = jnp.where(incl, gram[two:, :two], 0.0)
        a_rk = jnp.where(incl, gram[two:, two:], 0.0)
        h_t = h_ref[p]
        from_state = _dot_nt(lhs, h_t)
        v2 = stack(v)
        u2 = from_state[:two] + _dot(a_ak, v2)
        a_pow = a_ab
        span = 1
        while span < chunk:
            u2 = u2 + _dot(a_pow, u2)
            span *= 2
            if span < chunk:
                a_pow = _dot(a_pow, a_pow)
        y2 = from_state[two:] + _dot(a_rb, u2) + _dot(a_rk, v2)
        y = y2[:chunk] + y2[chunk:]
        uv = jnp.concatenate([u2, v2], axis=0)
        bk = jnp.concatenate([stack(b * to_end), stack(k * to_end)], axis=0)
        h_ref[p] = h_t * jnp.exp(log_p_end) + _dot_tn(uv, bk)

        mean = _dot_sel(y, block_ones) * (1.0 / HEAD_DIM)
        d = y - mean
        var = _dot_sel(d * d, block_ones) * (1.0 / HEAD_DIM)
        yn = d * lax.rsqrt(var + GN_EPS) * lnx_g[:, sl] + lnx_b[:, sl]
        bonus = _dot_sel(r * k * r_k[:, sl], block_ones) * v
        o_ref[:, sl] = ((yn + bonus) * g_ref[:, sl]).astype(o_ref.dtype)


def _wkv_scan(r, lw, k, v, a, b, g, vec, batch, seq):
    m = r.shape[0]
    assert 2 * WKV_CHUNK == LANES and seq % WKV_CHUNK == 0
    nc = seq // WKV_CHUNK
    row = lambda bi, c: (bi * nc + c, 0)
    blk = pl.BlockSpec((WKV_CHUNK, MIX_WIDTH), row)
    return pl.pallas_call(
        _wkv_kernel,
        out_shape=jax.ShapeDtypeStruct((m, MIX_WIDTH), BF16),
        grid=(batch, nc),
        in_specs=[blk] * 7 + [pl.BlockSpec(vec.shape, lambda bi, c: (0, 0))],
        out_specs=blk,
        scratch_shapes=[pltpu.VMEM((MIX_WIDTH // LANES, LANES, LANES), F32)],
        compiler_params=_params("arbitrary", "arbitrary"),
        name="wkv_scan",
    )(r, lw, k, v, a, b, g, vec)


def _mem_attn_kernel(q_ref, mk_ref, mv_ref, o_ref):
    q, mk, mv = q_ref[...], mk_ref[...], mv_ref[...]
    head_of_lane = _iota((1, MEM_WIDTH), 1) // HEAD_DIM
    acc = jnp.zeros(q.shape, F32)
    for h in range(MEM_WIDTH // HEAD_DIM):
        mine = head_of_lane == h
        s = _dot_nt(q, jnp.where(mine, mk, 0.0)) * (HEAD_DIM ** -0.5)
        s = s - jnp.max(s, axis=-1, keepdims=True)
        e = jnp.exp(s)
        p = e / jnp.sum(e, axis=-1, keepdims=True)
        acc = acc + _dot(p, jnp.where(mine, mv, 0.0))
    o_ref[...] = acc.astype(o_ref.dtype)


def _mem_attention(proj, q_col_block, mkv, batch, seq, mem_tokens):
    m = proj.shape[0]
    tq = min(512, seq)
    nq = seq // tq
    return pl.pallas_call(
        _mem_attn_kernel,
        out_shape=jax.ShapeDtypeStruct((m, MEM_WIDTH), BF16),
        grid=(batch, nq),
        in_specs=[pl.BlockSpec((tq, MEM_WIDTH), lambda b, i: (b * nq + i, q_col_block)),
                  pl.BlockSpec((mem_tokens, MEM_WIDTH), lambda b, i: (b, 0)),
                  pl.BlockSpec((mem_tokens, MEM_WIDTH), lambda b, i: (b, 1))],
        out_specs=pl.BlockSpec((tq, MEM_WIDTH), lambda b, i: (b * nq + i, 0)),
        compiler_params=_params("arbitrary", "arbitrary"),
        name="mem_attention",
    )(proj, mkv, mkv)


def _stick_kernel(q_ref, k_ref, v_ref, o_ref, k0, k1, v0, v1):
    i = pl.program_id(2)
    blk = q_ref.shape[0]
    head0 = _iota((1, LANES), 1) < HEAD_DIM

    @pl.when(i == 0)
    def _():
        kf, vf = k_ref[...], v_ref[...]
        k0[...] = jnp.where(head0, kf, 0.0).astype(BF16)
        k1[...] = jnp.where(head0, 0.0, kf).astype(BF16)
        v0[...] = jnp.where(head0, vf, 0.0).astype(BF16)
        v1[...] = jnp.where(head0, 0.0, vf).astype(BF16)

    q = q_ref[...].astype(BF16)
    scale = HEAD_DIM ** -0.5
    rr, cc = _iota((blk, blk), 0), _iota((blk, blk), 1)
    causal = cc < rr
    later = (rr > cc).astype(BF16)

    def block(kh, vh, j, carry, mask):
        rows = pl.ds(pl.multiple_of(j * blk, blk), blk)
        z = _dot_nt(q, kh[rows, :]) * scale
        sp = _softplus(z)
        log_1m = -sp if mask is None else jnp.where(mask, -sp, 0.0)
        log_rest = _dot_sel(log_1m, later) + carry
        w = jnp.exp(z - sp + log_rest)
        if mask is not None:
            w = jnp.where(mask, w, 0.0)
        return _dot(w, vh[rows, :]), carry + jnp.sum(log_1m, axis=-1, keepdims=True)

    acc = jnp.zeros((blk, LANES), F32)
    for kh, vh in ((k0, v0), (k1, v1)):
        out, carry = block(kh, vh, i, jnp.zeros((blk, 1), F32), causal)

        def body(n, state, kh=kh, vh=vh):
            out, carry = state
            o, carry = block(kh, vh, i - 1 - n, carry, None)
            return out + o, carry

        out, _ = lax.fori_loop(0, i, body, (out, carry))
        acc = acc + out
    o_ref[...] = acc.astype(o_ref.dtype)


def _stick_attention(proj, kv, batch, seq):
    m = proj.shape[0]
    nq = seq // ATT_BLOCK
    n_pairs = MIX_WIDTH // LANES
    return pl.pallas_call(
        _stick_kernel,
        out_shape=jax.ShapeDtypeStruct((m, MIX_WIDTH), BF16),
        grid=(batch, n_pairs, nq),
        in_specs=[pl.BlockSpec((ATT_BLOCK, LANES), lambda b, p, i: (b * nq + i, p)),
                  pl.BlockSpec((seq, LANES), lambda b, p, i: (b, p)),
                  pl.BlockSpec((seq, LANES), lambda b, p, i: (b, n_pairs + p))],
        out_specs=pl.BlockSpec((ATT_BLOCK, LANES), lambda b, p, i: (b * nq + i, p)),
        scratch_shapes=[pltpu.VMEM((seq, LANES), BF16)] * 4,
        compiler_params=_params("arbitrary", "arbitrary", "arbitrary"),
        name="stick_attention",
    )(proj, kv, kv)


def _mix_ln_kernel(tok_ref, mo_ref, x_ref, wa_ref, wb_ref, ln_ref, o_ref, ob_ref, *, alpha):
    mix = _dot(tok_ref[...], wa_ref[...]) + _dot(mo_ref[...], wb_ref[...])
    y = _layer_norm(alpha * x_ref[...] + mix, ln_ref[0:1, :], ln_ref[1:2, :])
    o_ref[...] = y
    ob_ref[...] = y.astype(BF16)


def _mix_ln(tok, mo, x, w_o, ln, alpha):
    m, d = x.shape
    tm = min(512, m)
    row = lambda i: (i, 0)
    return pl.pallas_call(
        functools.partial(_mix_ln_kernel, alpha=alpha),
        out_shape=(jax.ShapeDtypeStruct((m, d), F32), jax.ShapeDtypeStruct((m, d), BF16)),
        grid=(m // tm,),
        in_specs=[pl.BlockSpec((tm, MIX_WIDTH), row), pl.BlockSpec((tm, MEM_WIDTH), row),
                  pl.BlockSpec((tm, d), row),
                  pl.BlockSpec((MIX_WIDTH, d), lambda i: (0, 0)),
                  pl.BlockSpec((MEM_WIDTH, d), lambda i: (MIX_WIDTH // MEM_WIDTH, 0)),
                  pl.BlockSpec(ln.shape, lambda i: (0, 0))],
        out_specs=(pl.BlockSpec((tm, d), row),) * 2,
        compiler_params=_params("arbitrary"),
        name="mix_ln",
    )(tok, mo, x, w_o, w_o, ln)


def _router_kernel(x_ref, w_ref, o_ref, *, n_experts):
    logits = jnp.dot(x_ref[...], w_ref[...], preferred_element_type=F32, precision=lax.Precision.HIGHEST)
    lane = _iota(logits.shape, 1).astype(F32)
    neg = jnp.float32(-jnp.inf)
    logits = jnp.where(lane < n_experts, logits, neg)
    m1 = jnp.max(logits, axis=-1, keepdims=True)
    i1 = jnp.min(jnp.where(logits == m1, lane, float(LANES)), axis=-1, keepdims=True)
    rest = jnp.where(lane == i1, neg, logits)
    m2 = jnp.max(rest, axis=-1, keepdims=True)
    i2 = jnp.min(jnp.where(rest == m2, lane, float(LANES)), axis=-1, keepdims=True)
    e2 = jnp.exp(m2 - m1)
    w1 = 1.0 / (1.0 + e2)
    w2 = e2 / (1.0 + e2)
    o_ref[...] = jnp.where(lane == i1, w1, 0.0) + jnp.where(lane == i2, w2, 0.0)


def _router(x, w_router):
    m, d = x.shape
    n_experts = w_router.shape[1]
    w_pad = jnp.pad(w_router, ((0, 0), (0, LANES - n_experts)))
    tm = min(512, m)
    return pl.pallas_call(
        functools.partial(_router_kernel, n_experts=n_experts),
        out_shape=jax.ShapeDtypeStruct((m, LANES), F32),
        grid=(m // tm,),
        in_specs=[pl.BlockSpec((tm, d), lambda i: (i, 0)), pl.BlockSpec((d, LANES), lambda i: (0, 0))],
        out_specs=pl.BlockSpec((tm, LANES), lambda i: (i, 0)),
        compiler_params=_params("arbitrary"),
        name="router",
    )(x, w_pad)


def _ffn_up_kernel(*refs, weighted):
    if weighted:
        x_ref, wg_ref, wu_ref, comb_ref, o_ref = refs
    else:
        x_ref, wg_ref, wu_ref, o_ref = refs
    x = x_ref[...]
    gate = _dot(x, wg_ref[0])
    h = gate * _sigmoid(gate) * _dot(x, wu_ref[0])
    if weighted:
        comb = comb_ref[...]
        mine = _iota(comb.shape, 1) == pl.program_id(0)
        h = h * jnp.sum(jnp.where(mine, comb, 0.0), axis=-1, keepdims=True)
    o_ref[0] = h.astype(o_ref.dtype)


def _ffn_up(xb, w_gate_up, combine):
    m, d = xb.shape
    n_e = w_gate_up.shape[0]
    d_ff = w_gate_up.shape[2] // 2
    n_split = 2
    tn = d_ff // n_split
    tm = min(512, m)
    weighted = combine is not None
    in_specs = [pl.BlockSpec((tm, d), lambda e, n, i: (i, 0)),
                pl.BlockSpec((1, d, tn), lambda e, n, i: (e, 0, n)),
                pl.BlockSpec((1, d, tn), lambda e, n, i: (e, 0, n + n_split))]
    args = [xb, w_gate_up, w_gate_up]
    if weighted:
        in_specs.append(pl.BlockSpec((tm, LANES), lambda e, n, i: (i, 0)))
        args.append(combine)
    return pl.pallas_call(
        functools.partial(_ffn_up_kernel, weighted=weighted),
        out_shape=jax.ShapeDtypeStruct((n_e, m, d_ff), BF16),
        grid=(n_e, n_split, m // tm),
        in_specs=in_specs,
        out_specs=pl.BlockSpec((1, tm, tn), lambda e, n, i: (e, i, n)),
        compiler_params=_params("arbitrary", "arbitrary", "arbitrary"),
        name="ffn_up",
    )(*args)


def _ffn_down_kernel(h_ref, wd_ref, x_ref, ln_ref, o_ref, ob_ref, acc_ref, *, alpha):
    e = pl.program_id(1)

    @pl.when(e == 0)
    def _():
        acc_ref[...] = jnp.zeros_like(acc_ref)

    acc_ref[...] += _dot(h_ref[0], wd_ref[0])

    @pl.when(e == pl.num_programs(1) - 1)
    def _():
        y = _layer_norm(alpha * x_ref[...] + acc_ref[...], ln_ref[0:1, :], ln_ref[1:2, :])
        o_ref[...] = y
        ob_ref[...] = y.astype(BF16)


def _ffn_down(h, w_down, x, ln, alpha):
    n_e, m, d_ff = h.shape
    d = x.shape[1]
    tm = min(512, m)
    row = lambda i, e: (i, 0)
    return pl.pallas_call(
        functools.partial(_ffn_down_kernel, alpha=alpha),
        out_shape=(jax.ShapeDtypeStruct((m, d), F32), jax.ShapeDtypeStruct((m, d), BF16)),
        grid=(m // tm, n_e),
        in_specs=[pl.BlockSpec((1, tm, d_ff), lambda i, e: (e, i, 0)),
                  pl.BlockSpec((1, d_ff, d), lambda i, e: (e, 0, 0)),
                  pl.BlockSpec((tm, d), row),
                  pl.BlockSpec(ln.shape, lambda i, e: (0, 0))],
        out_specs=(pl.BlockSpec((tm, d), row),) * 2,
        scratch_shapes=[pltpu.VMEM((tm, d), F32)],
        compiler_params=_params("arbitrary", "arbitrary"),
        name="ffn_down",
    )(h, w_down, x, ln)


def _pad_rows(w, rows_before, rows_total):
    return jnp.pad(w, ((rows_before, rows_total - rows_before - w.shape[0]), (0, 0)))


def kernel(x, mem, a_w_in_first, a_w_in_rest, a_mu_first, a_mu_rest, a_vec, a_w_up, a_a_up, a_g_up, a_v0, a_v_up, a_r_k, b_w_in, w_kv_shared, mem_kv, w_o, ln, ffn_gate_up, ffn_down, router, exp_gate_up, exp_down):
    batch, seq, d_model = x.shape
    mem_tokens = mem.shape[1]
    depth = w_o.shape[0]
    n_a = a_vec.shape[0]
    alpha = (2 * depth) ** 0.25
    m = batch * seq
    rkv_w = 3 * MIX_WIDTH
    lora_w = DECAY_LORA + ICLR_LORA + GATE_LORA
    assert DECAY_LORA + ICLR_LORA == LANES and lora_w == 2 * LANES and rkv_w % lora_w == 0

    xf = x.reshape(m, d_model)
    xb = xf.astype(BF16)
    memb = mem.reshape(batch * mem_tokens, d_model).astype(BF16)
    hh = jnp.arange(MIX_WIDTH) // HEAD_DIM
    block_ones = (hh[:, None] == hh[None, :]).astype(BF16)

    v_first = None
    kv = None
    for l in range(depth):
        mkv = _matmul(memb, mem_kv[l].astype(BF16), tn=2 * MEM_WIDTH)
        if l < n_a:
            w_in = a_w_in_first if l == 0 else a_w_in_rest[l - 1]
            mu = a_mu_first if l == 0 else a_mu_rest[l - 1]
            has_vres = l > 0
            n_cols = w_in.shape[1] - MEM_WIDTH
            gate_lo = n_cols - GATE_LORA
            order = jnp.concatenate([jnp.arange(rkv_w + DECAY_LORA + ICLR_LORA), jnp.arange(gate_lo, n_cols),
                                     jnp.arange(n_cols, n_cols + MEM_WIDTH)])
            proj = _matmul(xb, w_in[:, order].astype(BF16), tn=(rkv_w + lora_w + MEM_WIDTH) // 2)
            mu_rkv = mu[:rkv_w].reshape(1, rkv_w)
            mu_lora = jnp.concatenate([mu[rkv_w:rkv_w + LANES], mu[gate_lo:n_cols]]).reshape(1, lora_w)
            ww = _pad_rows(a_w_up[l], 0, LANES).astype(BF16)
            wa = _pad_rows(a_a_up[l], DECAY_LORA, LANES).astype(BF16)
            wg = a_g_up[l].astype(BF16)
            vec = jnp.concatenate([a_vec[l, :4], a_v0[l - 1][None] if has_vres else jnp.zeros((1, MIX_WIDTH), F32),
                                   jnp.zeros((3, MIX_WIDTH), F32)])
            if has_vres:
                vd_lo = rkv_w + LANES
                w_vd = jnp.pad(w_in[:, vd_lo:vd_lo + VALUE_LORA], ((0, 0), (0, LANES - VALUE_LORA)))
                vd = _matmul(xb, w_vd.astype(BF16), tn=LANES)
                mu_vd = jnp.pad(mu[vd_lo:vd_lo + VALUE_LORA], (0, LANES - VALUE_LORA)).reshape(1, LANES)
                wv = _pad_rows(a_v_up[l - 1], 0, LANES).astype(BF16)
                ops = _rwkv_prep(proj, vd, v_first, mu_rkv, mu_lora, mu_vd, vec, ww, wa, wg, wv, block_ones,
                                 batch, seq)
            else:
                ops = _rwkv_prep(proj, None, None, mu_rkv, mu_lora, None, vec, ww, wa, wg, None, block_ones,
                                 batch, seq)
                v_first = ops[3]
            scan_vec = jnp.concatenate([a_vec[l, 4:6], a_r_k[l].reshape(1, MIX_WIDTH),
                                        jnp.zeros((5, MIX_WIDTH), F32)])
            tok = _wkv_scan(*ops, scan_vec, batch, seq)
            q_col_block = (rkv_w + lora_w) // MEM_WIDTH
        else:
            proj = _matmul(xb, b_w_in[l - n_a].astype(BF16), tn=(MIX_WIDTH + MEM_WIDTH) // 2)
            tok = _stick_attention(proj, kv, batch, seq)
            q_col_block = MIX_WIDTH // MEM_WIDTH
        mo = _mem_attention(proj, q_col_block, mkv, batch, seq, mem_tokens)
        xf, xb = _mix_ln(tok, mo, xf, w_o[l].astype(BF16), ln[l, 0:2], alpha)
        if l % 2 == 0:
            h = _ffn_up(xb, ffn_gate_up[l // 2][None].astype(BF16), None)
            xf, xb = _ffn_down(h, ffn_down[l // 2][None].astype(BF16), xf, ln[l, 2:4], alpha)
        else:
            combine = _router(xf, router[l // 2])
            h = _ffn_up(xb, exp_gate_up[l // 2].astype(BF16), combine)
            xf, xb = _ffn_down(h, exp_down[l // 2].astype(BF16), xf, ln[l, 2:4], alpha)
        if l == n_a - 1:
            kv = _matmul(xb, w_kv_shared.astype(BF16), tn=MIX_WIDTH)
    return xf.reshape(batch, seq, d_model)
```

```python
import functools

import jax
import jax.numpy as jnp
from jax import lax
from jax.experimental import pallas as pl
from jax.experimental.pallas import tpu as pltpu

F32 = jnp.float32
BF16 = jnp.bfloat16

HEAD_DIM = 64
LANES = 128
MIX_WIDTH = 768
MEM_WIDTH = 256
DECAY_LORA = 64
ICLR_LORA = 64
VALUE_LORA = 32
GATE_LORA = 128
TOP_K = 2
LN_EPS = 1e-5
GN_EPS = 64e-5
WKV_CHUNK = 64
ATT_BLOCK = 256
VMEM_LIMIT = 56 * 1024 * 1024


def _params(*sem):
    return pltpu.CompilerParams(dimension_semantics=sem, vmem_limit_bytes=VMEM_LIMIT)


def _dot(a, b):
    return jnp.dot(a.astype(BF16), b.astype(BF16), preferred_element_type=F32)


def _dot_nt(a, b):
    return lax.dot_general(a.astype(BF16), b.astype(BF16), (((1,), (1,)), ((), ())),
                           preferred_element_type=F32)


def _dot_tn(a, b):
    return lax.dot_general(a.astype(BF16), b.astype(BF16), (((0,), (0,)), ((), ())),
                           preferred_element_type=F32)


def _split3(x):
    x1 = x.astype(BF16)
    r1 = x - x1.astype(F32)
    x2 = r1.astype(BF16)
    x3 = (r1 - x2.astype(F32)).astype(BF16)
    return x1, x2, x3


def _dot_sel(x, sel):
    x1, x2, x3 = _split3(x)
    return _dot(x1, sel) + _dot(x2, sel) + _dot(x3, sel)


def _sel_dot(sel, x):
    x1, x2, x3 = _split3(x)
    return _dot(sel, x1) + _dot(sel, x2) + _dot(sel, x3)


def _softplus(z):
    return jnp.maximum(z, 0.0) + jnp.log1p(jnp.exp(-jnp.abs(z)))


def _sigmoid(z):
    return 1.0 / (1.0 + jnp.exp(-z))


def _layer_norm(y, g, b):
    mu = jnp.mean(y, axis=-1, keepdims=True)
    d = y - mu
    var = jnp.mean(d * d, axis=-1, keepdims=True)
    return d * lax.rsqrt(var + LN_EPS) * g + b


def _iota(shape, dim):
    return lax.broadcasted_iota(jnp.int32, shape, dim)


def _mm_kernel(x_ref, w_ref, o_ref):
    o_ref[...] = _dot(x_ref[...], w_ref[...]).astype(o_ref.dtype)


def _matmul(x, w, tn, out_dtype=F32, tm=512):
    m, k = x.shape
    n = w.shape[1]
    tm = min(tm, m)
    return pl.pallas_call(
        _mm_kernel,
        out_shape=jax.ShapeDtypeStruct((m, n), out_dtype),
        grid=(n // tn, m // tm),
        in_specs=[pl.BlockSpec((tm, k), lambda j, i: (i, 0)),
                  pl.BlockSpec((k, tn), lambda j, i: (0, j))],
        out_specs=pl.BlockSpec((tm, tn), lambda j, i: (i, j)),
        compiler_params=_params("arbitrary", "arbitrary"),
        name="matmul",
    )(x, w)


def _prep_kernel(*refs, has_vres):
    if has_vres:
        (rkv_ref, lora_ref, vd_ref, vf_ref, mu_rkv_ref, mu_lora_ref, mu_vd_ref, vec_ref,
         ww_ref, wa_ref, wg_ref, wv_ref, bo_ref,
         r_o, lw_o, k_o, v_o, a_o, b_o, g_o, c_rkv, c_lora, c_vd) = refs
    else:
        (rkv_ref, lora_ref, mu_rkv_ref, mu_lora_ref, vec_ref,
         ww_ref, wa_ref, wg_ref, bo_ref,
         r_o, lw_o, k_o, v_o, a_o, b_o, g_o, c_rkv, c_lora) = refs

    t = pl.program_id(1)
    carries = (c_rkv, c_lora) + ((c_vd,) if has_vres else ())

    @pl.when(t == 0)
    def _():
        for c in carries:
            c[...] = jnp.zeros_like(c)

    def shifted(x_ref, mu_ref, carry_ref):
        cur = x_ref[...]
        rows = cur.shape[0]
        rolled = pltpu.roll(cur, 1, 0)
        prev = jnp.where(_iota(cur.shape, 0) == 0, carry_ref[0:1, :], rolled)
        carry_ref[0:1, :] = cur[rows - 1:rows, :]
        return cur + mu_ref[...] * (prev - cur)

    xs = shifted(rkv_ref, mu_rkv_ref, c_rkv)
    xl = shifted(lora_ref, mu_lora_ref, c_lora)
    r = xs[:, :MIX_WIDTH]
    k = xs[:, MIX_WIDTH:2 * MIX_WIDTH]
    v = xs[:, 2 * MIX_WIDTH:]
    wa_in = xl[:, :LANES]
    gd = xl[:, LANES:]
    w0, a0, k_k, k_a = (vec_ref[i:i + 1, :] for i in range(4))

    log_w = -_softplus(-(w0 + _dot(jnp.tanh(wa_in), ww_ref[...]))) - 0.5
    lw_o[...] = -jnp.exp(log_w)
    a_lr = _sigmoid(a0 + _dot(wa_in, wa_ref[...]))
    if has_vres:
        xv = shifted(vd_ref, mu_vd_ref, c_vd)
        v0 = vec_ref[4:5, :]
        v = v + (vf_ref[...] - v) * _sigmoid(v0 + _dot(xv, wv_ref[...]))
    g_o[...] = _dot(_sigmoid(gd), wg_ref[...])
    kk = k * k_k
    norm = jnp.sqrt(_dot_sel(kk * kk, bo_ref[...]))
    kk = kk / jnp.maximum(norm, 1e-12)
    r_o[...] = r
    k_o[...] = k * (1.0 + (a_lr - 1.0) * k_a)
    v_o[...] = v
    a_o[...] = -kk
    b_o[...] = kk * a_lr


def _rwkv_prep(proj, vd, v_first, mu_rkv, mu_lora, mu_vd, vec, ww, wa, wg, wv, bo, batch, seq):
    has_vres = vd is not None
    m = proj.shape[0]
    tt = min(256, seq)
    nt = seq // tt
    rkv_w = 3 * MIX_WIDTH
    row = lambda b, t: (b * nt + t, 0)
    const = lambda b, t: (0, 0)
    full = lambda a: pl.BlockSpec(a.shape, const)
    in_specs = [pl.BlockSpec((tt, rkv_w), row),
                pl.BlockSpec((tt, 2 * LANES), lambda b, t: (b * nt + t, rkv_w // (2 * LANES)))]
    args = [proj, proj]
    if has_vres:
        in_specs += [pl.BlockSpec((tt, LANES), row), pl.BlockSpec((tt, MIX_WIDTH), row)]
        args += [vd, v_first]
    in_specs += [full(mu_rkv), full(mu_lora)]
    args += [mu_rkv, mu_lora]
    if has_vres:
        in_specs.append(full(mu_vd))
        args.append(mu_vd)
    in_specs += [full(vec), full(ww), full(wa), full(wg)]
    args += [vec, ww, wa, wg]
    if has_vres:
        in_specs.append(full(wv))
        args.append(wv)
    in_specs.append(full(bo))
    args.append(bo)
    scratch = [pltpu.VMEM((8, rkv_w), F32), pltpu.VMEM((8, 2 * LANES), F32)]
    if has_vres:
        scratch.append(pltpu.VMEM((8, LANES), F32))
    out = jax.ShapeDtypeStruct((m, MIX_WIDTH), F32)
    return pl.pallas_call(
        functools.partial(_prep_kernel, has_vres=has_vres),
        out_shape=(out,) * 7,
        grid=(batch, nt),
        in_specs=in_specs,
        out_specs=(pl.BlockSpec((tt, MIX_WIDTH), row),) * 7,
        scratch_shapes=scratch,
        compiler_params=_params("arbitrary", "arbitrary"),
        name="rwkv_prep",
    )(*args)


def _wkv_kernel(r_ref, lw_ref, k_ref, v_ref, a_ref, b_ref, g_ref, vec_ref, o_ref, h_ref):
    c = pl.program_id(1)
    chunk = r_ref.shape[0]
    two = 2 * chunk

    @pl.when(c == 0)
    def _():
        h_ref[...] = jnp.zeros_like(h_ref)

    tri = (_iota((chunk, chunk), 1) <= _iota((chunk, chunk), 0)).astype(BF16)
    log_p_all = _sel_dot(tri, lw_ref[...])
    head0 = _iota((1, LANES), 1) < HEAD_DIM
    rr, cc = _iota((two, two), 0), _iota((two, two), 1)
    strict, incl = cc < rr, cc <= rr
    block_ones = ((rr // HEAD_DIM) == (cc // HEAD_DIM)).astype(BF16)
    lnx_g, lnx_b, r_k = (vec_ref[i:i + 1, :] for i in range(3))

    def stack(x):
        return jnp.concatenate([jnp.where(head0, x, 0.0), jnp.where(head0, 0.0, x)], axis=0)

    pairs = range(r_ref.shape[1] // LANES)
    sls = [slice(p * LANES, (p + 1) * LANES) for p in pairs]
    lhs, rhs, bk, v2, decay_end, bonus = [], [], [], [], [], []
    for sl in sls:
        lw, log_p = lw_ref[:, sl], log_p_all[:, sl]
        log_p_end = log_p[chunk - 1:chunk, :]
        inv_p = jnp.exp(-log_p)
        to_end = jnp.exp(log_p_end - log_p)
        r, k, v, a, b = r_ref[:, sl], k_ref[:, sl], v_ref[:, sl], a_ref[:, sl], b_ref[:, sl]
        lhs.append(jnp.concatenate([stack(a * jnp.exp(log_p - lw)), stack(r * jnp.exp(log_p))], axis=0))
        rhs.append(jnp.concatenate([stack(b * inv_p), stack(k * inv_p)], axis=0))
        bk.append(jnp.concatenate([stack(b * to_end), stack(k * to_end)], axis=0))
        v2.append(stack(v))
        decay_end.append(jnp.exp(log_p_end))
        bonus.append((r * k * r_k[:, sl], v))
    gram = [_dot_nt(lhs[p], rhs[p]) for p in pairs]
    h_t = [h_ref[p] for p in pairs]
    from_state = [_dot_nt(lhs[p], h_t[p]) for p in pairs]
    a_pow = [jnp.where(strict, gram[p][:two, :two], 0.0) for p in pairs]
    u2 = [from_state[p][:two] + _dot(jnp.where(strict, gram[p][:two, two:], 0.0), v2[p]) for p in pairs]
    span = 1
    while span < chunk:
        u2 = [u2[p] + _dot(a_pow[p], u2[p]) for p in pairs]
        span *= 2
        if span < chunk:
            a_pow = [_dot(a_pow[p], a_pow[p]) for p in pairs]
    y2 = [from_state[p][two:] + _dot(jnp.where(incl, gram[p][two:, :two], 0.0), u2[p])
          + _dot(jnp.where(incl, gram[p][two:, two:], 0.0), v2[p]) for p in pairs]
    for p in pairs:
        h_ref[p] = h_t[p] * decay_end[p] + _dot_tn(jnp.concatenate([u2[p], v2[p]], axis=0), bk[p])

    y = [y2[p][:chunk] + y2[p][chunk:] for p in pairs]
    mean = [_dot_sel(y[p], block_ones) * (1.0 / HEAD_DIM) for p in pairs]
    d = [y[p] - mean[p] for p in pairs]
    var = [_dot_sel(d[p] * d[p], block_ones) * (1.0 / HEAD_DIM) for p in pairs]
    rk = [_dot_sel(bonus[p][0], block_ones) for p in pairs]
    for p, sl in zip(pairs, sls):
        yn = d[p] * lax.rsqrt(var[p] + GN_EPS) * lnx_g[:, sl] + lnx_b[:, sl]
        o_ref[:, sl] = ((yn + rk[p] * bonus[p][1]) * g_ref[:, sl]).astype(o_ref.dtype)


def _wkv_scan(r, lw, k, v, a, b, g, vec, batch, seq):
    m = r.shape[0]
    assert 2 * WKV_CHUNK == LANES and seq % WKV_CHUNK == 0
    nc = seq // WKV_CHUNK
    row = lambda bi, c: (bi * nc + c, 0)
    blk = pl.BlockSpec((WKV_CHUNK, MIX_WIDTH), row)
    return pl.pallas_call(
        _wkv_kernel,
        out_shape=jax.ShapeDtypeStruct((m, MIX_WIDTH), BF16),
        grid=(batch, nc),
        in_specs=[blk] * 7 + [pl.BlockSpec(vec.shape, lambda bi, c: (0, 0))],
        out_specs=blk,
        scratch_shapes=[pltpu.VMEM((MIX_WIDTH // LANES, LANES, LANES), F32)],
        compiler_params=_params("arbitrary", "arbitrary"),
        name="wkv_scan",
    )(r, lw, k, v, a, b, g, vec)


def _mem_attn_kernel(q_ref, mk_ref, mv_ref, o_ref):
    q, mk, mv = q_ref[...], mk_ref[...], mv_ref[...]
    head_of_lane = _iota((1, MEM_WIDTH), 1) // HEAD_DIM
    acc = jnp.zeros(q.shape, F32)
    for h in range(MEM_WIDTH // HEAD_DIM):
        mine = head_of_lane == h
        s = _dot_nt(q, jnp.where(mine, mk, 0.0)) * (HEAD_DIM ** -0.5)
        s = s - jnp.max(s, axis=-1, keepdims=True)
        e = jnp.exp(s)
        p = e / jnp.sum(e, axis=-1, keepdims=True)
        acc = acc + _dot(p, jnp.where(mine, mv, 0.0))
    o_ref[...] = acc.astype(o_ref.dtype)


def _mem_attention(proj, q_col_block, mkv, batch, seq, mem_tokens):
    m = proj.shape[0]
    tq = min(512, seq)
    nq = seq // tq
    return pl.pallas_call(
        _mem_attn_kernel,
        out_shape=jax.ShapeDtypeStruct((m, MEM_WIDTH), BF16),
        grid=(batch, nq),
        in_specs=[pl.BlockSpec((tq, MEM_WIDTH), lambda b, i: (b * nq + i, q_col_block)),
                  pl.BlockSpec((mem_tokens, MEM_WIDTH), lambda b, i: (b, 0)),
                  pl.BlockSpec((mem_tokens, MEM_WIDTH), lambda b, i: (b, 1))],
        out_specs=pl.BlockSpec((tq, MEM_WIDTH), lambda b, i: (b * nq + i, 0)),
        compiler_params=_params("arbitrary", "arbitrary"),
        name="mem_attention",
    )(proj, mkv, mkv)


def _stick_kernel(q_ref, k_ref, v_ref, o_ref, kt0, kt1, v0, v1, later_ref):
    i = pl.program_id(2)
    blk = q_ref.shape[0]
    seq = k_ref.shape[0]
    rr, cc = _iota((blk, blk), 0), _iota((blk, blk), 1)

    @pl.when(i == 0)
    def _():
        head0_lane = _iota((1, LANES), 1) < HEAD_DIM
        head0_row = _iota((LANES, 1), 0) < HEAD_DIM
        step = min(seq, 512)
        for c in range(seq // step):
            rows = slice(c * step, (c + 1) * step)
            kt = k_ref[rows, :].T
            kt0[:, rows] = jnp.where(head0_row, kt, 0.0).astype(BF16)
            kt1[:, rows] = jnp.where(head0_row, 0.0, kt).astype(BF16)
            vf = v_ref[rows, :]
            v0[rows, :] = jnp.where(head0_lane, vf, 0.0).astype(BF16)
            v1[rows, :] = jnp.where(head0_lane, 0.0, vf).astype(BF16)
        later_ref[...] = (rr > cc).astype(BF16)

    q = (q_ref[...] * (HEAD_DIM ** -0.5)).astype(BF16)
    causal = cc < rr

    def blocks(js, state, mask):
        acc, carries = state[0], list(state[1:])
        chains = [(j, h) for j in range(len(js)) for h in (0, 1)]
        keys = [pl.ds(pl.multiple_of(j * blk, blk), blk) for j in js]
        z = [_dot(q, (kt0, kt1)[h][:, keys[j]]) for j, h in chains]
        sp = [jnp.maximum(x, 0.0) + jnp.log(1.0 + jnp.exp(-jnp.abs(x))) for x in z]
        log_1m = [-x if mask is None else jnp.where(mask, -x, 0.0) for x in sp]
        hi = [x.astype(BF16) for x in log_1m]
        lo = [(x - y.astype(F32)).astype(BF16) for x, y in zip(log_1m, hi)]
        later = later_ref[...]
        within = [_dot(x, later) + _dot(y, later) for x, y in zip(hi, lo)]
        sums = [jnp.sum(x, axis=-1, keepdims=True) for x in log_1m]
        for n, (j, h) in enumerate(chains):
            w = jnp.exp(z[n] - sp[n] + within[n] + carries[h])
            if mask is not None:
                w = jnp.where(mask, w, 0.0)
            acc = acc + _dot(w, (v0, v1)[h][keys[j], :])
            carries[h] = carries[h] + sums[n]
        return (acc, *carries)

    zero = jnp.zeros((blk, 1), F32)
    state = blocks([i], (jnp.zeros((blk, LANES), F32), zero, zero), causal)
    state = lax.fori_loop(0, i // 2, lambda n, s: blocks([i - 1 - 2 * n, i - 2 - 2 * n], s, None), state)
    state = lax.fori_loop(0, i % 2, lambda n, s: blocks([0], s, None), state)
    o_ref[...] = state[0].astype(o_ref.dtype)


def _stick_attention(proj, kv, batch, seq):
    m = proj.shape[0]
    blk = min(ATT_BLOCK, seq)
    nq = seq // blk
    n_pairs = MIX_WIDTH // LANES
    return pl.pallas_call(
        _stick_kernel,
        out_shape=jax.ShapeDtypeStruct((m, MIX_WIDTH), BF16),
        grid=(batch, n_pairs, nq),
        in_specs=[pl.BlockSpec((blk, LANES), lambda b, p, i: (b * nq + i, p)),
                  pl.BlockSpec((seq, LANES), lambda b, p, i: (b, p)),
                  pl.BlockSpec((seq, LANES), lambda b, p, i: (b, n_pairs + p))],
        out_specs=pl.BlockSpec((blk, LANES), lambda b, p, i: (b * nq + i, p)),
        scratch_shapes=[pltpu.VMEM((LANES, seq), BF16)] * 2 + [pltpu.VMEM((seq, LANES), BF16)] * 2
                       + [pltpu.VMEM((blk, blk), BF16)],
        compiler_params=_params("arbitrary", "arbitrary", "arbitrary"),
        name="stick_attention",
    )(proj, kv, kv)


def _mix_ln_kernel(tok_ref, mo_ref, x_ref, wa_ref, wb_ref, ln_ref, o_ref, ob_ref, *, alpha):
    mix = _dot(tok_ref[...], wa_ref[...]) + _dot(mo_ref[...], wb_ref[...])
    y = _layer_norm(alpha * x_ref[...] + mix, ln_ref[0:1, :], ln_ref[1:2, :])
    o_ref[...] = y
    ob_ref[...] = y.astype(BF16)


def _mix_ln(tok, mo, x, w_o, ln, alpha):
    m, d = x.shape
    tm = min(512, m)
    row = lambda i: (i, 0)
    return pl.pallas_call(
        functools.partial(_mix_ln_kernel, alpha=alpha),
        out_shape=(jax.ShapeDtypeStruct((m, d), F32), jax.ShapeDtypeStruct((m, d), BF16)),
        grid=(m // tm,),
        in_specs=[pl.BlockSpec((tm, MIX_WIDTH), row), pl.BlockSpec((tm, MEM_WIDTH), row),
                  pl.BlockSpec((tm, d), row),
                  pl.BlockSpec((MIX_WIDTH, d), lambda i: (0, 0)),
                  pl.BlockSpec((MEM_WIDTH, d), lambda i: (MIX_WIDTH // MEM_WIDTH, 0)),
                  pl.BlockSpec(ln.shape, lambda i: (0, 0))],
        out_specs=(pl.BlockSpec((tm, d), row),) * 2,
        compiler_params=_params("arbitrary"),
        name="mix_ln",
    )(tok, mo, x, w_o, w_o, ln)


def _router_kernel(x_ref, w_ref, o_ref, *, n_experts):
    logits = jnp.dot(x_ref[...], w_ref[...], preferred_element_type=F32, precision=lax.Precision.HIGHEST)
    lane = _iota(logits.shape, 1).astype(F32)
    neg = jnp.float32(-jnp.inf)
    logits = jnp.where(lane < n_experts, logits, neg)
    m1 = jnp.max(logits, axis=-1, keepdims=True)
    i1 = jnp.min(jnp.where(logits == m1, lane, float(LANES)), axis=-1, keepdims=True)
    rest = jnp.where(lane == i1, neg, logits)
    m2 = jnp.max(rest, axis=-1, keepdims=True)
    i2 = jnp.min(jnp.where(rest == m2, lane, float(LANES)), axis=-1, keepdims=True)
    e2 = jnp.exp(m2 - m1)
    w1 = 1.0 / (1.0 + e2)
    w2 = e2 / (1.0 + e2)
    o_ref[...] = jnp.where(lane == i1, w1, 0.0) + jnp.where(lane == i2, w2, 0.0)


def _router(x, w_router):
    m, d = x.shape
    n_experts = w_router.shape[1]
    w_pad = jnp.pad(w_router, ((0, 0), (0, LANES - n_experts)))
    tm = min(512, m)
    return pl.pallas_call(
        functools.partial(_router_kernel, n_experts=n_experts),
        out_shape=jax.ShapeDtypeStruct((m, LANES), F32),
        grid=(m // tm,),
        in_specs=[pl.BlockSpec((tm, d), lambda i: (i, 0)), pl.BlockSpec((d, LANES), lambda i: (0, 0))],
        out_specs=pl.BlockSpec((tm, LANES), lambda i: (i, 0)),
        compiler_params=_params("arbitrary"),
        name="router",
    )(x, w_pad)


def _ffn_up_kernel(*refs, weighted):
    if weighted:
        x_ref, wg_ref, wu_ref, comb_ref, o_ref = refs
    else:
        x_ref, wg_ref, wu_ref, o_ref = refs
    x = x_ref[...]
    gate = _dot(x, wg_ref[0])
    h = gate * _sigmoid(gate) * _dot(x, wu_ref[0])
    if weighted:
        comb = comb_ref[...]
        mine = _iota(comb.shape, 1) == pl.program_id(0)
        h = h * jnp.sum(jnp.where(mine, comb, 0.0), axis=-1, keepdims=True)
    o_ref[0] = h.astype(o_ref.dtype)


def _ffn_up(xb, w_gate_up, combine):
    m, d = xb.shape
    n_e = w_gate_up.shape[0]
    d_ff = w_gate_up.shape[2] // 2
    n_split = 2
    tn = d_ff // n_split
    tm = min(512, m)
    weighted = combine is not None
    in_specs = [pl.BlockSpec((tm, d), lambda e, n, i: (i, 0)),
                pl.BlockSpec((1, d, tn), lambda e, n, i: (e, 0, n)),
                pl.BlockSpec((1, d, tn), lambda e, n, i: (e, 0, n + n_split))]
    args = [xb, w_gate_up, w_gate_up]
    if weighted:
        in_specs.append(pl.BlockSpec((tm, LANES), lambda e, n, i: (i, 0)))
        args.append(combine)
    return pl.pallas_call(
        functools.partial(_ffn_up_kernel, weighted=weighted),
        out_shape=jax.ShapeDtypeStruct((n_e, m, d_ff), BF16),
        grid=(n_e, n_split, m // tm),
        in_specs=in_specs,
        out_specs=pl.BlockSpec((1, tm, tn), lambda e, n, i: (e, i, n)),
        compiler_params=_params("arbitrary", "arbitrary", "arbitrary"),
        name="ffn_up",
    )(*args)


def _ffn_down_kernel(h_ref, wd_ref, x_ref, ln_ref, o_ref, ob_ref, acc_ref, *, alpha):
    e = pl.program_id(1)

    @pl.when(e == 0)
    def _():
        acc_ref[...] = jnp.zeros_like(acc_ref)

    acc_ref[...] += _dot(h_ref[0], wd_ref[0])

    @pl.when(e == pl.num_programs(1) - 1)
    def _():
        y = _layer_norm(alpha * x_ref[...] + acc_ref[...], ln_ref[0:1, :], ln_ref[1:2, :])
        o_ref[...] = y
        ob_ref[...] = y.astype(BF16)


def _ffn_down(h, w_down, x, ln, alpha):
    n_e, m, d_ff = h.shape
    d = x.shape[1]
    tm = min(512, m)
    row = lambda i, e: (i, 0)
    return pl.pallas_call(
        functools.partial(_ffn_down_kernel, alpha=alpha),
        out_shape=(jax.ShapeDtypeStruct((m, d), F32), jax.ShapeDtypeStruct((m, d), BF16)),
        grid=(m // tm, n_e),
        in_specs=[pl.BlockSpec((1, tm, d_ff), lambda i, e: (e, i, 0)),
                  pl.BlockSpec((1, d_ff, d), lambda i, e: (e, 0, 0)),
                  pl.BlockSpec((tm, d), row),
                  pl.BlockSpec(ln.shape, lambda i, e: (0, 0))],
        out_specs=(pl.BlockSpec((tm, d), row),) * 2,
        scratch_shapes=[pltpu.VMEM((tm, d), F32)],
        compiler_params=_params("arbitrary", "arbitrary"),
        name="ffn_down",
    )(h, w_down, x, ln)


def _pad_rows(w, rows_before, rows_total):
    return jnp.pad(w, ((rows_before, rows_total - rows_before - w.shape[0]), (0, 0)))


def kernel(x, mem, a_w_in_first, a_w_in_rest, a_mu_first, a_mu_rest, a_vec, a_w_up, a_a_up, a_g_up, a_v0, a_v_up, a_r_k, b_w_in, w_kv_shared, mem_kv, w_o, ln, ffn_gate_up, ffn_down, router, exp_gate_up, exp_down):
    batch, seq, d_model = x.shape
    mem_tokens = mem.shape[1]
    depth = w_o.shape[0]
    n_a = a_vec.shape[0]
    alpha = (2 * depth) ** 0.25
    m = batch * seq
    rkv_w = 3 * MIX_WIDTH
    lora_w = DECAY_LORA + ICLR_LORA + GATE_LORA
    assert DECAY_LORA + ICLR_LORA == LANES and lora_w == 2 * LANES and rkv_w % lora_w == 0

    xf = x.reshape(m, d_model)
    xb = xf.astype(BF16)
    memb = mem.reshape(batch * mem_tokens, d_model).astype(BF16)
    hh = jnp.arange(MIX_WIDTH) // HEAD_DIM
    block_ones = (hh[:, None] == hh[None, :]).astype(BF16)

    v_first = None
    kv = None
    for l in range(depth):
        mkv = _matmul(memb, mem_kv[l].astype(BF16), tn=2 * MEM_WIDTH)
        if l < n_a:
            w_in = a_w_in_first if l == 0 else a_w_in_rest[l - 1]
            mu = a_mu_first if l == 0 else a_mu_rest[l - 1]
            has_vres = l > 0
            n_cols = w_in.shape[1] - MEM_WIDTH
            gate_lo = n_cols - GATE_LORA
            order = jnp.concatenate([jnp.arange(rkv_w + DECAY_LORA + ICLR_LORA), jnp.arange(gate_lo, n_cols),
                                     jnp.arange(n_cols, n_cols + MEM_WIDTH)])
            proj = _matmul(xb, w_in[:, order].astype(BF16), tn=(rkv_w + lora_w + MEM_WIDTH) // 2)
            mu_rkv = mu[:rkv_w].reshape(1, rkv_w)
            mu_lora = jnp.concatenate([mu[rkv_w:rkv_w + LANES], mu[gate_lo:n_cols]]).reshape(1, lora_w)
            ww = _pad_rows(a_w_up[l], 0, LANES).astype(BF16)
            wa = _pad_rows(a_a_up[l], DECAY_LORA, LANES).astype(BF16)
            wg = a_g_up[l].astype(BF16)
            vec = jnp.concatenate([a_vec[l, :4], a_v0[l - 1][None] if has_vres else jnp.zeros((1, MIX_WIDTH), F32),
                                   jnp.zeros((3, MIX_WIDTH), F32)])
            if has_vres:
                vd_lo = rkv_w + LANES
                w_vd = jnp.pad(w_in[:, vd_lo:vd_lo + VALUE_LORA], ((0, 0), (0, LANES - VALUE_LORA)))
                vd = _matmul(xb, w_vd.astype(BF16), tn=LANES)
                mu_vd = jnp.pad(mu[vd_lo:vd_lo + VALUE_LORA], (0, LANES - VALUE_LORA)).reshape(1, LANES)
                wv = _pad_rows(a_v_up[l - 1], 0, LANES).astype(BF16)
                ops = _rwkv_prep(proj, vd, v_first, mu_rkv, mu_lora, mu_vd, vec, ww, wa, wg, wv, block_ones,
                                 batch, seq)
            else:
                ops = _rwkv_prep(proj, None, None, mu_rkv, mu_lora, None, vec, ww, wa, wg, None, block_ones,
                                 batch, seq)
                v_first = ops[3]
            scan_vec = jnp.concatenate([a_vec[l, 4:6], a_r_k[l].reshape(1, MIX_WIDTH),
                                        jnp.zeros((5, MIX_WIDTH), F32)])
            tok = _wkv_scan(*ops, scan_vec, batch, seq)
            q_col_block = (rkv_w + lora_w) // MEM_WIDTH
        else:
            proj = _matmul(xb, b_w_in[l - n_a].astype(BF16), tn=(MIX_WIDTH + MEM_WIDTH) // 2)
            tok = _stick_attention(proj, kv, batch, seq)
            q_col_block = MIX_WIDTH // MEM_WIDTH
        mo = _mem_attention(proj, q_col_block, mkv, batch, seq, mem_tokens)
        xf, xb = _mix_ln(tok, mo, xf, w_o[l].astype(BF16), ln[l, 0:2], alpha)
        if l % 2 == 0:
            h = _ffn_up(xb, ffn_gate_up[l // 2][None].astype(BF16), None)
            xf, xb = _ffn_down(h, ffn_down[l // 2][None].astype(BF16), xf, ln[l, 2:4], alpha)
        else:
            combine = _router(xf, router[l // 2])
            h = _ffn_up(xb, exp_gate_up[l // 2].astype(BF16), combine)
            xf, xb = _ffn_down(h, exp_down[l // 2].astype(BF16), xf, ln[l, 2:4], alpha)
        if l == n_a - 1:
            kv = _matmul(xb, w_kv_shared.astype(BF16), tn=MIX_WIDTH)
    return xf.reshape(batch, seq, d_model)
```

```python
import functools

import jax
import jax.numpy as jnp
from jax import lax
from jax.experimental import pallas as pl
from jax.experimental.pallas import tpu as pltpu

F32 = jnp.float32
BF16 = jnp.bfloat16

HEAD_DIM = 64
LANES = 128
MIX_WIDTH = 768
MEM_WIDTH = 256
DECAY_LORA = 64
ICLR_LORA = 64
VALUE_LORA = 32
GATE_LORA = 128
TOP_K = 2
LN_EPS = 1e-5
GN_EPS = 64e-5
WKV_CHUNK = 64
ATT_BLOCK = 256
MOE_TILE = 512
COMBINE_TILE = 256
VMEM_LIMIT = 56 * 1024 * 1024


def _params(*sem):
    return pltpu.CompilerParams(dimension_semantics=sem, vmem_limit_bytes=VMEM_LIMIT)


def _dot(a, b):
    return jnp.dot(a.astype(BF16), b.astype(BF16), preferred_element_type=F32)


def _dot_nt(a, b):
    return lax.dot_general(a.astype(BF16), b.astype(BF16), (((1,), (1,)), ((), ())),
                           preferred_element_type=F32)


def _dot_tn(a, b):
    return lax.dot_general(a.astype(BF16), b.astype(BF16), (((0,), (0,)), ((), ())),
                           preferred_element_type=F32)


def _split3(x):
    x1 = x.astype(BF16)
    r1 = x - x1.astype(F32)
    x2 = r1.astype(BF16)
    x3 = (r1 - x2.astype(F32)).astype(BF16)
    return x1, x2, x3


def _dot_sel(x, sel):
    x1, x2, x3 = _split3(x)
    return _dot(x1, sel) + _dot(x2, sel) + _dot(x3, sel)


def _sel_dot(sel, x):
    x1, x2, x3 = _split3(x)
    return _dot(sel, x1) + _dot(sel, x2) + _dot(sel, x3)


def _softplus(z):
    return jnp.maximum(z, 0.0) + jnp.log1p(jnp.exp(-jnp.abs(z)))


def _sigmoid(z):
    return 1.0 / (1.0 + jnp.exp(-z))


def _layer_norm(y, g, b):
    mu = jnp.mean(y, axis=-1, keepdims=True)
    d = y - mu
    var = jnp.mean(d * d, axis=-1, keepdims=True)
    return d * lax.rsqrt(var + LN_EPS) * g + b


def _iota(shape, dim):
    return lax.broadcasted_iota(jnp.int32, shape, dim)


def _mm_kernel(x_ref, w_ref, o_ref):
    o_ref[...] = _dot(x_ref[...], w_ref[...]).astype(o_ref.dtype)


def _matmul(x, w, tn, out_dtype=F32, tm=512):
    m, k = x.shape
    n = w.shape[1]
    tm = min(tm, m)
    return pl.pallas_call(
        _mm_kernel,
        out_shape=jax.ShapeDtypeStruct((m, n), out_dtype),
        grid=(n // tn, m // tm),
        in_specs=[pl.BlockSpec((tm, k), lambda j, i: (i, 0)),
                  pl.BlockSpec((k, tn), lambda j, i: (0, j))],
        out_specs=pl.BlockSpec((tm, tn), lambda j, i: (i, j)),
        compiler_params=_params("arbitrary", "arbitrary"),
        name="matmul",
    )(x, w)


def _prep_kernel(*refs, has_vres):
    if has_vres:
        (rkv_ref, lora_ref, vd_ref, vf_ref, mu_rkv_ref, mu_lora_ref, mu_vd_ref, vec_ref,
         ww_ref, wa_ref, wg_ref, wv_ref, bo_ref,
         r_o, lw_o, k_o, v_o, a_o, b_o, g_o, c_rkv, c_lora, c_vd) = refs
    else:
        (rkv_ref, lora_ref, mu_rkv_ref, mu_lora_ref, vec_ref,
         ww_ref, wa_ref, wg_ref, bo_ref,
         r_o, lw_o, k_o, v_o, a_o, b_o, g_o, c_rkv, c_lora) = refs

    t = pl.program_id(1)
    carries = (c_rkv, c_lora) + ((c_vd,) if has_vres else ())

    @pl.when(t == 0)
    def _():
        for c in carries:
            c[...] = jnp.zeros_like(c)

    def shifted(x_ref, mu_ref, carry_ref):
        cur = x_ref[...]
        rows = cur.shape[0]
        rolled = pltpu.roll(cur, 1, 0)
        prev = jnp.where(_iota(cur.shape, 0) == 0, carry_ref[0:1, :], rolled)
        carry_ref[0:1, :] = cur[rows - 1:rows, :]
        return cur + mu_ref[...] * (prev - cur)

    xs = shifted(rkv_ref, mu_rkv_ref, c_rkv)
    xl = shifted(lora_ref, mu_lora_ref, c_lora)
    r = xs[:, :MIX_WIDTH]
    k = xs[:, MIX_WIDTH:2 * MIX_WIDTH]
    v = xs[:, 2 * MIX_WIDTH:]
    wa_in = xl[:, :LANES]
    gd = xl[:, LANES:]
    w0, a0, k_k, k_a = (vec_ref[i:i + 1, :] for i in range(4))

    log_w = -_softplus(-(w0 + _dot(jnp.tanh(wa_in), ww_ref[...]))) - 0.5
    lw_o[...] = -jnp.exp(log_w)
    a_lr = _sigmoid(a0 + _dot(wa_in, wa_ref[...]))
    if has_vres:
        xv = shifted(vd_ref, mu_vd_ref, c_vd)
        v0 = vec_ref[4:5, :]
        v = v + (vf_ref[...] - v) * _sigmoid(v0 + _dot(xv, wv_ref[...]))
    g_o[...] = _dot(_sigmoid(gd), wg_ref[...])
    kk = k * k_k
    norm = jnp.sqrt(_dot_sel(kk * kk, bo_ref[...]))
    kk = kk / jnp.maximum(norm, 1e-12)
    r_o[...] = r
    k_o[...] = k * (1.0 + (a_lr - 1.0) * k_a)
    v_o[...] = v
    a_o[...] = -kk
    b_o[...] = kk * a_lr


def _rwkv_prep(proj, vd, v_first, mu_rkv, mu_lora, mu_vd, vec, ww, wa, wg, wv, bo, batch, seq):
    has_vres = vd is not None
    m = proj.shape[0]
    tt = min(256, seq)
    nt = seq // tt
    rkv_w = 3 * MIX_WIDTH
    row = lambda b, t: (b * nt + t, 0)
    const = lambda b, t: (0, 0)
    full = lambda a: pl.BlockSpec(a.shape, const)
    in_specs = [pl.BlockSpec((tt, rkv_w), row),
                pl.BlockSpec((tt, 2 * LANES), lambda b, t: (b * nt + t, rkv_w // (2 * LANES)))]
    args = [proj, proj]
    if has_vres:
        in_specs += [pl.BlockSpec((tt, LANES), row), pl.BlockSpec((tt, MIX_WIDTH), row)]
        args += [vd, v_first]
    in_specs += [full(mu_rkv), full(mu_lora)]
    args += [mu_rkv, mu_lora]
    if has_vres:
        in_specs.append(full(mu_vd))
        args.append(mu_vd)
    in_specs += [full(vec), full(ww), full(wa), full(wg)]
    args += [vec, ww, wa, wg]
    if has_vres:
        in_specs.append(full(wv))
        args.append(wv)
    in_specs.append(full(bo))
    args.append(bo)
    scratch = [pltpu.VMEM((8, rkv_w), F32), pltpu.VMEM((8, 2 * LANES), F32)]
    if has_vres:
        scratch.append(pltpu.VMEM((8, LANES), F32))
    out = jax.ShapeDtypeStruct((m, MIX_WIDTH), F32)
    return pl.pallas_call(
        functools.partial(_prep_kernel, has_vres=has_vres),
        out_shape=(out,) * 7,
        grid=(batch, nt),
        in_specs=in_specs,
        out_specs=(pl.BlockSpec((tt, MIX_WIDTH), row),) * 7,
        scratch_shapes=scratch,
        compiler_params=_params("arbitrary", "arbitrary"),
        name="rwkv_prep",
    )(*args)


def _wkv_kernel(r_ref, lw_ref, k_ref, v_ref, a_ref, b_ref, g_ref, vec_ref, o_ref, h_ref):
    c = pl.program_id(1)
    chunk = r_ref.shape[0]
    two = 2 * chunk

    @pl.when(c == 0)
    def _():
        h_ref[...] = jnp.zeros_like(h_ref)

    tri = (_iota((chunk, chunk), 1) <= _iota((chunk, chunk), 0)).astype(BF16)
    log_p_all = _sel_dot(tri, lw_ref[...])
    head0 = _iota((1, LANES), 1) < HEAD_DIM
    rr, cc = _iota((two, two), 0), _iota((two, two), 1)
    strict, incl = cc < rr, cc <= rr
    block_ones = ((rr // HEAD_DIM) == (cc // HEAD_DIM)).astype(BF16)
    lnx_g, lnx_b, r_k = (vec_ref[i:i + 1, :] for i in range(3))

    def stack(x):
        return jnp.concatenate([jnp.where(head0, x, 0.0), jnp.where(head0, 0.0, x)], axis=0)

    pairs = range(r_ref.shape[1] // LANES)
    sls = [slice(p * LANES, (p + 1) * LANES) for p in pairs]
    lhs, rhs, bk, v2, decay_end, bonus = [], [], [], [], [], []
    for sl in sls:
        lw, log_p = lw_ref[:, sl], log_p_all[:, sl]
        log_p_end = log_p[chunk - 1:chunk, :]
        inv_p = jnp.exp(-log_p)
        to_end = jnp.exp(log_p_end - log_p)
        r, k, v, a, b = r_ref[:, sl], k_ref[:, sl], v_ref[:, sl], a_ref[:, sl], b_ref[:, sl]
        lhs.append(jnp.concatenate([stack(a * jnp.exp(log_p - lw)), stack(r * jnp.exp(log_p))], axis=0))
        rhs.append(jnp.concatenate([stack(b * inv_p), stack(k * inv_p)], axis=0))
        bk.append(jnp.concatenate([stack(b * to_end), stack(k * to_end)], axis=0))
        v2.append(stack(v))
        decay_end.append(jnp.exp(log_p_end))
        bonus.append((r * k * r_k[:, sl], v))
    gram = [_dot_nt(lhs[p], rhs[p]) for p in pairs]
    h_t = [h_ref[p] for p in pairs]
    from_state = [_dot_nt(lhs[p], h_t[p]) for p in pairs]
    a_pow = [jnp.where(strict, gram[p][:two, :two], 0.0) for p in pairs]
    u2 = [from_state[p][:two] + _dot(jnp.where(strict, gram[p][:two, two:], 0.0), v2[p]) for p in pairs]
    span = 1
    while span < chunk:
        u2 = [u2[p] + _dot(a_pow[p], u2[p]) for p in pairs]
        span *= 2
        if span < chunk:
            a_pow = [_dot(a_pow[p], a_pow[p]) for p in pairs]
    y2 = [from_state[p][two:] + _dot(jnp.where(incl, gram[p][two:, :two], 0.0), u2[p])
          + _dot(jnp.where(incl, gram[p][two:, two:], 0.0), v2[p]) for p in pairs]
    for p in pairs:
        h_ref[p] = h_t[p] * decay_end[p] + _dot_tn(jnp.concatenate([u2[p], v2[p]], axis=0), bk[p])

    y = [y2[p][:chunk] + y2[p][chunk:] for p in pairs]
    mean = [_dot_sel(y[p], block_ones) * (1.0 / HEAD_DIM) for p in pairs]
    d = [y[p] - mean[p] for p in pairs]
    var = [_dot_sel(d[p] * d[p], block_ones) * (1.0 / HEAD_DIM) for p in pairs]
    rk = [_dot_sel(bonus[p][0], block_ones) for p in pairs]
    for p, sl in zip(pairs, sls):
        yn = d[p] * lax.rsqrt(var[p] + GN_EPS) * lnx_g[:, sl] + lnx_b[:, sl]
        o_ref[:, sl] = ((yn + rk[p] * bonus[p][1]) * g_ref[:, sl]).astype(o_ref.dtype)


def _wkv_scan(r, lw, k, v, a, b, g, vec, batch, seq):
    m = r.shape[0]
    assert 2 * WKV_CHUNK == LANES and seq % WKV_CHUNK == 0
    nc = seq // WKV_CHUNK
    row = lambda bi, c: (bi * nc + c, 0)
    blk = pl.BlockSpec((WKV_CHUNK, MIX_WIDTH), row)
    return pl.pallas_call(
        _wkv_kernel,
        out_shape=jax.ShapeDtypeStruct((m, MIX_WIDTH), BF16),
        grid=(batch, nc),
        in_specs=[blk] * 7 + [pl.BlockSpec(vec.shape, lambda bi, c: (0, 0))],
        out_specs=blk,
        scratch_shapes=[pltpu.VMEM((MIX_WIDTH // LANES, LANES, LANES), F32)],
        compiler_params=_params("arbitrary", "arbitrary"),
        name="wkv_scan",
    )(r, lw, k, v, a, b, g, vec)


def _mem_attn_kernel(q_ref, mk_ref, mv_ref, o_ref):
    q, mk, mv = q_ref[...], mk_ref[...], mv_ref[...]
    head_of_lane = _iota((1, MEM_WIDTH), 1) // HEAD_DIM
    acc = jnp.zeros(q.shape, F32)
    for h in range(MEM_WIDTH // HEAD_DIM):
        mine = head_of_lane == h
        s = _dot_nt(q, jnp.where(mine, mk, 0.0)) * (HEAD_DIM ** -0.5)
        s = s - jnp.max(s, axis=-1, keepdims=True)
        e = jnp.exp(s)
        p = e / jnp.sum(e, axis=-1, keepdims=True)
        acc = acc + _dot(p, jnp.where(mine, mv, 0.0))
    o_ref[...] = acc.astype(o_ref.dtype)


def _mem_attention(proj, q_col_block, mkv, batch, seq, mem_tokens):
    m = proj.shape[0]
    tq = min(512, seq)
    nq = seq // tq
    return pl.pallas_call(
        _mem_attn_kernel,
        out_shape=jax.ShapeDtypeStruct((m, MEM_WIDTH), BF16),
        grid=(batch, nq),
        in_specs=[pl.BlockSpec((tq, MEM_WIDTH), lambda b, i: (b * nq + i, q_col_block)),
                  pl.BlockSpec((mem_tokens, MEM_WIDTH), lambda b, i: (b, 0)),
                  pl.BlockSpec((mem_tokens, MEM_WIDTH), lambda b, i: (b, 1))],
        out_specs=pl.BlockSpec((tq, MEM_WIDTH), lambda b, i: (b * nq + i, 0)),
        compiler_params=_params("arbitrary", "arbitrary"),
        name="mem_attention",
    )(proj, mkv, mkv)


def _stick_kernel(q_ref, k_ref, v_ref, o_ref, kt0, kt1, v0, v1, later_ref):
    i = pl.program_id(2)
    blk = q_ref.shape[0]
    seq = k_ref.shape[0]
    rr, cc = _iota((blk, blk), 0), _iota((blk, blk), 1)

    @pl.when(i == 0)
    def _():
        head0_lane = _iota((1, LANES), 1) < HEAD_DIM
        head0_row = _iota((LANES, 1), 0) < HEAD_DIM
        step = min(seq, 512)
        for c in range(seq // step):
            rows = slice(c * step, (c + 1) * step)
            kt = k_ref[rows, :].T
            kt0[:, rows] = jnp.where(head0_row, kt, 0.0).astype(BF16)
            kt1[:, rows] = jnp.where(head0_row, 0.0, kt).astype(BF16)
            vf = v_ref[rows, :]
            v0[rows, :] = jnp.where(head0_lane, vf, 0.0).astype(BF16)
            v1[rows, :] = jnp.where(head0_lane, 0.0, vf).astype(BF16)
        later_ref[...] = (rr > cc).astype(BF16)

    q = (q_ref[...] * (HEAD_DIM ** -0.5)).astype(BF16)
    causal = cc < rr

    def blocks(js, state, mask):
        acc, carries = state[0], list(state[1:])
        chains = [(j, h) for j in range(len(js)) for h in (0, 1)]
        keys = [pl.ds(pl.multiple_of(j * blk, blk), blk) for j in js]
        z = [_dot(q, (kt0, kt1)[h][:, keys[j]]) for j, h in chains]
        sp = [jnp.maximum(x, 0.0) + jnp.log(1.0 + jnp.exp(-jnp.abs(x))) for x in z]
        log_1m = [-x if mask is None else jnp.where(mask, -x, 0.0) for x in sp]
        hi = [x.astype(BF16) for x in log_1m]
        lo = [(x - y.astype(F32)).astype(BF16) for x, y in zip(log_1m, hi)]
        later = later_ref[...]
        within = [_dot(x, later) + _dot(y, later) for x, y in zip(hi, lo)]
        sums = [jnp.sum(x, axis=-1, keepdims=True) for x in log_1m]
        for n, (j, h) in enumerate(chains):
            w = jnp.exp(z[n] - sp[n] + within[n] + carries[h])
            if mask is not None:
                w = jnp.where(mask, w, 0.0)
            acc = acc + _dot(w, (v0, v1)[h][keys[j], :])
            carries[h] = carries[h] + sums[n]
        return (acc, *carries)

    zero = jnp.zeros((blk, 1), F32)
    state = blocks([i], (jnp.zeros((blk, LANES), F32), zero, zero), causal)
    state = lax.fori_loop(0, i // 2, lambda n, s: blocks([i - 1 - 2 * n, i - 2 - 2 * n], s, None), state)
    state = lax.fori_loop(0, i % 2, lambda n, s: blocks([0], s, None), state)
    o_ref[...] = state[0].astype(o_ref.dtype)


def _stick_attention(proj, kv, batch, seq):
    m = proj.shape[0]
    blk = min(ATT_BLOCK, seq)
    nq = seq // blk
    n_pairs = MIX_WIDTH // LANES
    return pl.pallas_call(
        _stick_kernel,
        out_shape=jax.ShapeDtypeStruct((m, MIX_WIDTH), BF16),
        grid=(batch, n_pairs, nq),
        in_specs=[pl.BlockSpec((blk, LANES), lambda b, p, i: (b * nq + i, p)),
                  pl.BlockSpec((seq, LANES), lambda b, p, i: (b, p)),
                  pl.BlockSpec((seq, LANES), lambda b, p, i: (b, n_pairs + p))],
        out_specs=pl.BlockSpec((blk, LANES), lambda b, p, i: (b * nq + i, p)),
        scratch_shapes=[pltpu.VMEM((LANES, seq), BF16)] * 2 + [pltpu.VMEM((seq, LANES), BF16)] * 2
                       + [pltpu.VMEM((blk, blk), BF16)],
        compiler_params=_params("arbitrary", "arbitrary", "arbitrary"),
        name="stick_attention",
    )(proj, kv, kv)


def _mix_ln_kernel(tok_ref, mo_ref, x_ref, wa_ref, wb_ref, ln_ref, o_ref, ob_ref, *, alpha):
    mix = _dot(tok_ref[...], wa_ref[...]) + _dot(mo_ref[...], wb_ref[...])
    y = _layer_norm(alpha * x_ref[...] + mix, ln_ref[0:1, :], ln_ref[1:2, :])
    o_ref[...] = y
    ob_ref[...] = y.astype(BF16)


def _mix_ln(tok, mo, x, w_o, ln, alpha):
    m, d = x.shape
    tm = min(512, m)
    row = lambda i: (i, 0)
    return pl.pallas_call(
        functools.partial(_mix_ln_kernel, alpha=alpha),
        out_shape=(jax.ShapeDtypeStruct((m, d), F32), jax.ShapeDtypeStruct((m, d), BF16)),
        grid=(m // tm,),
        in_specs=[pl.BlockSpec((tm, MIX_WIDTH), row), pl.BlockSpec((tm, MEM_WIDTH), row),
                  pl.BlockSpec((tm, d), row),
                  pl.BlockSpec((MIX_WIDTH, d), lambda i: (0, 0)),
                  pl.BlockSpec((MEM_WIDTH, d), lambda i: (MIX_WIDTH // MEM_WIDTH, 0)),
                  pl.BlockSpec(ln.shape, lambda i: (0, 0))],
        out_specs=(pl.BlockSpec((tm, d), row),) * 2,
        compiler_params=_params("arbitrary"),
        name="mix_ln",
    )(tok, mo, x, w_o, w_o, ln)


HI16 = -65536


def _pack_bf16_pairs(x):
    half = x.shape[1] // 2
    xb = x.astype(BF16).astype(F32)
    lo = lax.bitcast_convert_type(xb[:, :half], jnp.int32)
    hi = lax.bitcast_convert_type(xb[:, half:], jnp.int32)
    return (hi & HI16) | lax.shift_right_logical(lo, 16)


def _unpack_bf16_pairs(u):
    lo = lax.bitcast_convert_type(lax.shift_left(u, 16), F32)
    hi = lax.bitcast_convert_type(u & HI16, F32)
    return jnp.concatenate([lo, hi], axis=-1).astype(BF16)


def _router_kernel(x_ref, w_ref, comb_ref, info_ref, xpk_ref, cnt_ref, carry_ref, *, n_experts):
    @pl.when(pl.program_id(0) == 0)
    def _():
        carry_ref[...] = jnp.zeros_like(carry_ref)

    x = x_ref[...]
    logits = jnp.dot(x, w_ref[...], preferred_element_type=F32, precision=lax.Precision.HIGHEST)
    lane = _iota(logits.shape, 1).astype(F32)
    neg = jnp.float32(-jnp.inf)
    logits = jnp.where(lane < n_experts, logits, neg)
    m1 = jnp.max(logits, axis=-1, keepdims=True)
    i1 = jnp.min(jnp.where(logits == m1, lane, float(LANES)), axis=-1, keepdims=True)
    rest = jnp.where(lane == i1, neg, logits)
    m2 = jnp.max(rest, axis=-1, keepdims=True)
    i2 = jnp.min(jnp.where(rest == m2, lane, float(LANES)), axis=-1, keepdims=True)
    e2 = jnp.exp(m2 - m1)
    w1 = 1.0 / (1.0 + e2)
    w2 = e2 / (1.0 + e2)
    first, second = lane == i1, lane == i2
    comb_ref[...] = jnp.where(first, w1, 0.0) + jnp.where(second, w2, 0.0)

    sel = jnp.where(first, 1.0, 0.0) + jnp.where(second, 1.0, 0.0)
    rows = sel.shape[0]
    earlier = (_iota((rows, rows), 1) < _iota((rows, rows), 0)).astype(BF16)
    rank = _dot(earlier, sel) + carry_ref[0:1, :]
    total = carry_ref[0:1, :] + jnp.sum(sel, axis=0, keepdims=True)
    carry_ref[0:1, :] = total
    cnt_ref[...] = jnp.broadcast_to(total, cnt_ref.shape)
    rank1 = jnp.sum(jnp.where(first, rank, 0.0), axis=-1, keepdims=True)
    rank2 = jnp.sum(jnp.where(second, rank, 0.0), axis=-1, keepdims=True)
    info_ref[...] = jnp.where(lane == 0, i1, jnp.where(lane == 1, i2, jnp.where(lane == 2, rank1,
                              jnp.where(lane == 3, rank2, 0.0))))
    xpk_ref[...] = _pack_bf16_pairs(x)


def _router(x, w_router):
    m, d = x.shape
    n_experts = w_router.shape[1]
    w_pad = jnp.pad(w_router, ((0, 0), (0, LANES - n_experts)))
    tm = min(512, m)
    row = lambda i: (i, 0)
    return pl.pallas_call(
        functools.partial(_router_kernel, n_experts=n_experts),
        out_shape=(jax.ShapeDtypeStruct((m, LANES), F32), jax.ShapeDtypeStruct((m, LANES), F32),
                   jax.ShapeDtypeStruct((m, d // 2), jnp.int32), jax.ShapeDtypeStruct((8, LANES), F32)),
        grid=(m // tm,),
        in_specs=[pl.BlockSpec((tm, d), row), pl.BlockSpec((d, LANES), lambda i: (0, 0))],
        out_specs=(pl.BlockSpec((tm, LANES), row), pl.BlockSpec((tm, LANES), row),
                   pl.BlockSpec((tm, d // 2), row), pl.BlockSpec((8, LANES), lambda i: (0, 0))),
        scratch_shapes=[pltpu.VMEM((8, LANES), F32)],
        compiler_params=_params("arbitrary"),
        name="router",
    )(x, w_pad)


def _moe_src_kernel(pos_ref, src_ref):
    def clear(i, c):
        src_ref[i] = 0
        return c

    lax.fori_loop(0, src_ref.shape[0], clear, 0, unroll=16)

    def place(t, c):
        src_ref[pos_ref[2 * t]] = t
        src_ref[pos_ref[2 * t + 1]] = t
        return c

    lax.fori_loop(0, pos_ref.shape[0] // 2, place, 0, unroll=8)


def _moe_src(pos, n_rows):
    smem = pl.BlockSpec(memory_space=pltpu.SMEM)
    return pl.pallas_call(
        _moe_src_kernel,
        out_shape=jax.ShapeDtypeStruct((n_rows,), jnp.int32),
        in_specs=[smem],
        out_specs=smem,
        name="moe_src",
    )(pos)


def _moe_dispatch_kernel(src_ref, xpk_ref, comb_ref, xs_ref, ws_ref):
    rows = xs_ref.shape[0]
    base = pl.program_id(0) * rows

    def copy(i, c):
        t = src_ref[base + i]
        xs_ref[pl.ds(i, 1), :] = xpk_ref[pl.ds(t, 1), :]
        ws_ref[pl.ds(i, 1), :] = comb_ref[pl.ds(t, 1), :]
        return c

    lax.fori_loop(0, rows, copy, 0, unroll=8)


def _moe_dispatch(src, xpk, comb, n_tiles):
    m, half = xpk.shape
    whole = lambda a: pl.BlockSpec(a.shape, lambda r, s: (0, 0), pipeline_mode=pl.Buffered(1))
    tile = lambda w: pl.BlockSpec((MOE_TILE, w), lambda r, s: (r, 0))
    return pl.pallas_call(
        _moe_dispatch_kernel,
        out_shape=(jax.ShapeDtypeStruct((n_tiles * MOE_TILE, half), jnp.int32),
                   jax.ShapeDtypeStruct((n_tiles * MOE_TILE, LANES), F32)),
        grid_spec=pltpu.PrefetchScalarGridSpec(
            num_scalar_prefetch=1, grid=(n_tiles,),
            in_specs=[whole(xpk), whole(comb)],
            out_specs=(tile(half), tile(LANES))),
        compiler_params=_params("arbitrary"),
        name="moe_dispatch",
    )(src, xpk, comb)


def _moe_up_kernel(te_ref, tv_ref, xs_ref, ws_ref, wg_ref, wu_ref, o_ref):
    r = pl.program_id(1)

    @pl.when(tv_ref[r] == 1)
    def _():
        x = _unpack_bf16_pairs(xs_ref[...])
        gate = _dot(x, wg_ref[0])
        h = gate * _sigmoid(gate) * _dot(x, wu_ref[0])
        ws = ws_ref[...]
        mine = _iota(ws.shape, 1) == te_ref[r]
        o_ref[...] = (h * jnp.sum(jnp.where(mine, ws, 0.0), axis=-1, keepdims=True)).astype(o_ref.dtype)

    @pl.when(tv_ref[r] == 0)
    def _():
        o_ref[...] = jnp.zeros_like(o_ref)


def _moe_up(tile_expert, tile_valid, xs, ws, w_gate_up):
    n_rows, half = xs.shape
    d = 2 * half
    d_ff = w_gate_up.shape[2] // 2
    n_split = 2
    tn = d_ff // n_split
    return pl.pallas_call(
        _moe_up_kernel,
        out_shape=jax.ShapeDtypeStruct((n_rows, d_ff), BF16),
        grid_spec=pltpu.PrefetchScalarGridSpec(
            num_scalar_prefetch=2, grid=(n_split, n_rows // MOE_TILE),
            in_specs=[pl.BlockSpec((MOE_TILE, half), lambda n, r, te, tv: (r, 0)),
                      pl.BlockSpec((MOE_TILE, LANES), lambda n, r, te, tv: (r, 0)),
                      pl.BlockSpec((1, d, tn), lambda n, r, te, tv: (te[r], 0, n)),
                      pl.BlockSpec((1, d, tn), lambda n, r, te, tv: (te[r], 0, n + n_split))],
            out_specs=pl.BlockSpec((MOE_TILE, tn), lambda n, r, te, tv: (r, n))),
        compiler_params=_params("arbitrary", "arbitrary"),
        name="moe_up",
    )(tile_expert, tile_valid, xs, ws, w_gate_up, w_gate_up)


def _moe_down_kernel(te_ref, tv_ref, h_ref, wd_ref, o_ref):
    r = pl.program_id(0)

    @pl.when(tv_ref[r] == 1)
    def _():
        o_ref[...] = _dot(h_ref[...], wd_ref[0]).astype(o_ref.dtype)

    @pl.when(tv_ref[r] == 0)
    def _():
        o_ref[...] = jnp.zeros_like(o_ref)


def _moe_down(tile_expert, tile_valid, h, w_down):
    n_rows, d_ff = h.shape
    d = w_down.shape[2]
    return pl.pallas_call(
        _moe_down_kernel,
        out_shape=jax.ShapeDtypeStruct((n_rows, d), BF16),
        grid_spec=pltpu.PrefetchScalarGridSpec(
            num_scalar_prefetch=2, grid=(n_rows // MOE_TILE,),
            in_specs=[pl.BlockSpec((MOE_TILE, d_ff), lambda r, te, tv: (r, 0)),
                      pl.BlockSpec((1, d_ff, d), lambda r, te, tv: (te[r], 0, 0))],
            out_specs=pl.BlockSpec((MOE_TILE, d), lambda r, te, tv: (r, 0))),
        compiler_params=_params("arbitrary"),
        name="moe_down",
    )(tile_expert, tile_valid, h, w_down)


def _moe_combine_kernel(win_ref, ok_ref, rs_ref, info_ref, x_ref, ln_ref, *refs, alpha, n_experts):
    n_win = 2 * n_experts
    windows, (o_ref, ob_ref) = refs[:n_win], refs[n_win:]
    i = pl.program_id(0)
    info = info_ref[...]
    i1, i2, rank1, rank2 = (info[:, c:c + 1] for c in range(4))
    pos1, pos2 = rank1, rank2
    for e in range(n_experts):
        start = rs_ref[e].astype(F32)
        pos1 = pos1 + jnp.where(i1 == e, start, 0.0)
        pos2 = pos2 + jnp.where(i2 == e, start, 0.0)
    rows = windows[0].shape[0]
    lane = _iota((1, rows), 1).astype(F32)
    acc = None
    for k in range(n_win):
        base = jnp.where(ok_ref[i * n_win + k] == 1, win_ref[i * n_win + k] * rows, -rows).astype(F32)
        pick = jnp.where(pos1 - base == lane, 1.0, jnp.where(pos2 - base == lane, 1.0, 0.0))
        part = _dot(pick, windows[k][...])
        acc = part if acc is None else acc + part
    y = _layer_norm(alpha * x_ref[...] + acc, ln_ref[0:1, :], ln_ref[1:2, :])
    o_ref[...] = y
    ob_ref[...] = y.astype(BF16)


def _moe_combine(win, ok, row_start, info, x, ln, out, alpha, n_experts):
    m, d = x.shape
    tt = COMBINE_TILE
    n_win = 2 * n_experts
    row = lambda i, *_: (i, 0)
    window = lambda k: pl.BlockSpec((tt, d), lambda i, w, o, s: (w[i * n_win + k], 0))
    return pl.pallas_call(
        functools.partial(_moe_combine_kernel, alpha=alpha, n_experts=n_experts),
        out_shape=(jax.ShapeDtypeStruct((m, d), F32), jax.ShapeDtypeStruct((m, d), BF16)),
        grid_spec=pltpu.PrefetchScalarGridSpec(
            num_scalar_prefetch=3, grid=(m // tt,),
            in_specs=[pl.BlockSpec((tt, LANES), row), pl.BlockSpec((tt, d), row),
                      pl.BlockSpec(ln.shape, lambda i, *_: (0, 0))] + [window(k) for k in range(n_win)],
            out_specs=(pl.BlockSpec((tt, d), row),) * 2),
        compiler_params=_params("arbitrary"),
        name="moe_combine",
    )(win, ok, row_start, info, x, ln, *([out] * n_win))


def _moe_swiglu(x, w_router, w_gate_up, w_down, ln, alpha):
    m, d = x.shape
    n_experts = w_router.shape[1]
    assert (TOP_K * m) % MOE_TILE == 0 and m % COMBINE_TILE == 0 and MOE_TILE % COMBINE_TILE == 0
    n_tiles = TOP_K * m // MOE_TILE + n_experts
    comb, info, xpk, counts = _router(x, w_router)

    i32 = jnp.int32
    experts = jnp.arange(n_experts, dtype=i32)
    counts = counts[0, :n_experts].astype(i32)
    tiles = (counts + MOE_TILE - 1) // MOE_TILE
    tile_end = jnp.cumsum(tiles)
    row_start = (tile_end - tiles) * MOE_TILE
    t = jnp.arange(n_tiles, dtype=i32)
    tile_expert = jnp.minimum(jnp.sum(t[:, None] >= tile_end[None, :], axis=1), n_experts - 1).astype(i32)
    tile_valid = (t < tile_end[-1]).astype(i32)
    i1, i2 = info[:, 0].astype(i32), info[:, 1].astype(i32)
    hit1, hit2 = i1[:, None] == experts[None, :], i2[:, None] == experts[None, :]
    pos1 = info[:, 2].astype(i32) + jnp.sum(jnp.where(hit1, row_start[None, :], 0), axis=1)
    pos2 = info[:, 3].astype(i32) + jnp.sum(jnp.where(hit2, row_start[None, :], 0), axis=1)
    per_tile = jnp.sum((hit1 | hit2).astype(i32).reshape(m // COMBINE_TILE, COMBINE_TILE, n_experts), axis=1)
    first = row_start[None, :] + jnp.cumsum(per_tile, axis=0) - per_tile
    win0 = first // COMBINE_TILE
    win1 = (first + jnp.maximum(per_tile, 1) - 1) // COMBINE_TILE
    ok0 = per_tile > 0
    ok1 = ok0 & (win1 != win0)
    win = jnp.stack([jnp.where(ok0, win0, 0), jnp.where(ok1, win1, 0)], axis=-1).reshape(-1).astype(i32)
    ok = jnp.stack([ok0, ok1], axis=-1).reshape(-1).astype(i32)

    src = _moe_src(jnp.stack([pos1, pos2], axis=-1).reshape(-1), n_tiles * MOE_TILE)
    xs, ws = _moe_dispatch(src, xpk, comb, n_tiles)
    h = _moe_up(tile_expert, tile_valid, xs, ws, w_gate_up)
    out = _moe_down(tile_expert, tile_valid, h, w_down)
    return _moe_combine(win, ok, row_start, info, x, ln, out, alpha, n_experts)


def _ffn_up_kernel(x_ref, wg_ref, wu_ref, o_ref):
    x = x_ref[...]
    gate = _dot(x, wg_ref[0])
    o_ref[0] = (gate * _sigmoid(gate) * _dot(x, wu_ref[0])).astype(o_ref.dtype)


def _ffn_up(xb, w_gate_up):
    m, d = xb.shape
    n_e = w_gate_up.shape[0]
    d_ff = w_gate_up.shape[2] // 2
    n_split = 2
    tn = d_ff // n_split
    tm = min(512, m)
    return pl.pallas_call(
        _ffn_up_kernel,
        out_shape=jax.ShapeDtypeStruct((n_e, m, d_ff), BF16),
        grid=(n_e, n_split, m // tm),
        in_specs=[pl.BlockSpec((tm, d), lambda e, n, i: (i, 0)),
                  pl.BlockSpec((1, d, tn), lambda e, n, i: (e, 0, n)),
                  pl.BlockSpec((1, d, tn), lambda e, n, i: (e, 0, n + n_split))],
        out_specs=pl.BlockSpec((1, tm, tn), lambda e, n, i: (e, i, n)),
        compiler_params=_params("arbitrary", "arbitrary", "arbitrary"),
        name="ffn_up",
    )(xb, w_gate_up, w_gate_up)


def _ffn_down_kernel(h_ref, wd_ref, x_ref, ln_ref, o_ref, ob_ref, acc_ref, *, alpha):
    e = pl.program_id(1)

    @pl.when(e == 0)
    def _():
        acc_ref[...] = jnp.zeros_like(acc_ref)

    acc_ref[...] += _dot(h_ref[0], wd_ref[0])

    @pl.when(e == pl.num_programs(1) - 1)
    def _():
        y = _layer_norm(alpha * x_ref[...] + acc_ref[...], ln_ref[0:1, :], ln_ref[1:2, :])
        o_ref[...] = y
        ob_ref[...] = y.astype(BF16)


def _ffn_down(h, w_down, x, ln, alpha):
    n_e, m, d_ff = h.shape
    d = x.shape[1]
    tm = min(512, m)
    row = lambda i, e: (i, 0)
    return pl.pallas_call(
        functools.partial(_ffn_down_kernel, alpha=alpha),
        out_shape=(jax.ShapeDtypeStruct((m, d), F32), jax.ShapeDtypeStruct((m, d), BF16)),
        grid=(m // tm, n_e),
        in_specs=[pl.BlockSpec((1, tm, d_ff), lambda i, e: (e, i, 0)),
                  pl.BlockSpec((1, d_ff, d), lambda i, e: (e, 0, 0)),
                  pl.BlockSpec((tm, d), row),
                  pl.BlockSpec(ln.shape, lambda i, e: (0, 0))],
        out_specs=(pl.BlockSpec((tm, d), row),) * 2,
        scratch_shapes=[pltpu.VMEM((tm, d), F32)],
        compiler_params=_params("arbitrary", "arbitrary"),
        name="ffn_down",
    )(h, w_down, x, ln)


def _pad_rows(w, rows_before, rows_total):
    return jnp.pad(w, ((rows_before, rows_total - rows_before - w.shape[0]), (0, 0)))


def kernel(x, mem, a_w_in_first, a_w_in_rest, a_mu_first, a_mu_rest, a_vec, a_w_up, a_a_up, a_g_up, a_v0, a_v_up, a_r_k, b_w_in, w_kv_shared, mem_kv, w_o, ln, ffn_gate_up, ffn_down, router, exp_gate_up, exp_down):
    batch, seq, d_model = x.shape
    mem_tokens = mem.shape[1]
    depth = w_o.shape[0]
    n_a = a_vec.shape[0]
    alpha = (2 * depth) ** 0.25
    m = batch * seq
    rkv_w = 3 * MIX_WIDTH
    lora_w = DECAY_LORA + ICLR_LORA + GATE_LORA
    assert DECAY_LORA + ICLR_LORA == LANES and lora_w == 2 * LANES and rkv_w % lora_w == 0

    xf = x.reshape(m, d_model)
    xb = xf.astype(BF16)
    memb = mem.reshape(batch * mem_tokens, d_model).astype(BF16)
    hh = jnp.arange(MIX_WIDTH) // HEAD_DIM
    block_ones = (hh[:, None] == hh[None, :]).astype(BF16)

    v_first = None
    kv = None
    for l in range(depth):
        mkv = _matmul(memb, mem_kv[l].astype(BF16), tn=2 * MEM_WIDTH)
        if l < n_a:
            w_in = a_w_in_first if l == 0 else a_w_in_rest[l - 1]
            mu = a_mu_first if l == 0 else a_mu_rest[l - 1]
            has_vres = l > 0
            n_cols = w_in.shape[1] - MEM_WIDTH
            gate_lo = n_cols - GATE_LORA
            order = jnp.concatenate([jnp.arange(rkv_w + DECAY_LORA + ICLR_LORA), jnp.arange(gate_lo, n_cols),
                                     jnp.arange(n_cols, n_cols + MEM_WIDTH)])
            proj = _matmul(xb, w_in[:, order].astype(BF16), tn=(rkv_w + lora_w + MEM_WIDTH) // 2)
            mu_rkv = mu[:rkv_w].reshape(1, rkv_w)
            mu_lora = jnp.concatenate([mu[rkv_w:rkv_w + LANES], mu[gate_lo:n_cols]]).reshape(1, lora_w)
            ww = _pad_rows(a_w_up[l], 0, LANES).astype(BF16)
            wa = _pad_rows(a_a_up[l], DECAY_LORA, LANES).astype(BF16)
            wg = a_g_up[l].astype(BF16)
            vec = jnp.concatenate([a_vec[l, :4], a_v0[l - 1][None] if has_vres else jnp.zeros((1, MIX_WIDTH), F32),
                                   jnp.zeros((3, MIX_WIDTH), F32)])
            if has_vres:
                vd_lo = rkv_w + LANES
                w_vd = jnp.pad(w_in[:, vd_lo:vd_lo + VALUE_LORA], ((0, 0), (0, LANES - VALUE_LORA)))
                vd = _matmul(xb, w_vd.astype(BF16), tn=LANES)
                mu_vd = jnp.pad(mu[vd_lo:vd_lo + VALUE_LORA], (0, LANES - VALUE_LORA)).reshape(1, LANES)
                wv = _pad_rows(a_v_up[l - 1], 0, LANES).astype(BF16)
                ops = _rwkv_prep(proj, vd, v_first, mu_rkv, mu_lora, mu_vd, vec, ww, wa, wg, wv, block_ones,
                                 batch, seq)
            else:
                ops = _rwkv_prep(proj, None, None, mu_rkv, mu_lora, None, vec, ww, wa, wg, None, block_ones,
                                 batch, seq)
                v_first = ops[3]
            scan_vec = jnp.concatenate([a_vec[l, 4:6], a_r_k[l].reshape(1, MIX_WIDTH),
                                        jnp.zeros((5, MIX_WIDTH), F32)])
            tok = _wkv_scan(*ops, scan_vec, batch, seq)
            q_col_block = (rkv_w + lora_w) // MEM_WIDTH
        else:
            proj = _matmul(xb, b_w_in[l - n_a].astype(BF16), tn=(MIX_WIDTH + MEM_WIDTH) // 2)
            tok = _stick_attention(proj, kv, batch, seq)
            q_col_block = MIX_WIDTH // MEM_WIDTH
        mo = _mem_attention(proj, q_col_block, mkv, batch, seq, mem_tokens)
        xf, xb = _mix_ln(tok, mo, xf, w_o[l].astype(BF16), ln[l, 0:2], alpha)
        if l % 2 == 0:
            h = _ffn_up(xb, ffn_gate_up[l // 2][None].astype(BF16))
            xf, xb = _ffn_down(h, ffn_down[l // 2][None].astype(BF16), xf, ln[l, 2:4], alpha)
        else:
            xf, xb = _moe_swiglu(xf, router[l // 2], exp_gate_up[l // 2].astype(BF16),
                                 exp_down[l // 2].astype(BF16), ln[l, 2:4], alpha)
        if l == n_a - 1:
            kv = _matmul(xb, w_kv_shared.astype(BF16), tn=MIX_WIDTH)
    return xf.reshape(batch, seq, d_model)
```

```python
import functools

import jax
import jax.numpy as jnp
from jax import lax
from jax.experimental import pallas as pl
from jax.experimental.pallas import tpu as pltpu

F32 = jnp.float32
BF16 = jnp.bfloat16

HEAD_DIM = 64
LANES = 128
MIX_WIDTH = 768
MEM_WIDTH = 256
DECAY_LORA = 64
ICLR_LORA = 64
VALUE_LORA = 32
GATE_LORA = 128
TOP_K = 2
LN_EPS = 1e-5
GN_EPS = 64e-5
WKV_CHUNK = 64
ATT_BLOCK = 256
LOG2E = 1.4426950408889634
SIGN_BIT = -2147483648
MOE_TILE = 512
COMBINE_TILE = 256
VMEM_LIMIT = 56 * 1024 * 1024


def _params(*sem):
    return pltpu.CompilerParams(dimension_semantics=sem, vmem_limit_bytes=VMEM_LIMIT)


def _dot(a, b):
    return jnp.dot(a.astype(BF16), b.astype(BF16), preferred_element_type=F32)


def _dot_nt(a, b):
    return lax.dot_general(a.astype(BF16), b.astype(BF16), (((1,), (1,)), ((), ())),
                           preferred_element_type=F32)


def _dot_tn(a, b):
    return lax.dot_general(a.astype(BF16), b.astype(BF16), (((0,), (0,)), ((), ())),
                           preferred_element_type=F32)


def _split3(x):
    x1 = x.astype(BF16)
    r1 = x - x1.astype(F32)
    x2 = r1.astype(BF16)
    x3 = (r1 - x2.astype(F32)).astype(BF16)
    return x1, x2, x3


def _dot_sel(x, sel):
    x1 = x.astype(BF16)
    x2 = (x - x1.astype(F32)).astype(BF16)
    return _dot(x1, sel) + _dot(x2, sel)


def _sel_dot(sel, x):
    x1, x2, x3 = _split3(x)
    return _dot(sel, x1) + _dot(sel, x2) + _dot(sel, x3)


def _softplus(z):
    return jnp.maximum(z, 0.0) + jnp.log1p(jnp.exp(-jnp.abs(z)))


def _sigmoid(z):
    return 1.0 / (1.0 + jnp.exp(-z))


def _layer_norm(y, g, b):
    mu = jnp.mean(y, axis=-1, keepdims=True)
    d = y - mu
    var = jnp.mean(d * d, axis=-1, keepdims=True)
    return d * lax.rsqrt(var + LN_EPS) * g + b


def _iota(shape, dim):
    return lax.broadcasted_iota(jnp.int32, shape, dim)


def _mm_kernel(x_ref, w_ref, o_ref):
    o_ref[...] = _dot(x_ref[...], w_ref[...]).astype(o_ref.dtype)


def _matmul(x, w, tn, out_dtype=F32, tm=512):
    m, k = x.shape
    n = w.shape[1]
    tm = min(tm, m)
    return pl.pallas_call(
        _mm_kernel,
        out_shape=jax.ShapeDtypeStruct((m, n), out_dtype),
        grid=(n // tn, m // tm),
        in_specs=[pl.BlockSpec((tm, k), lambda j, i: (i, 0)),
                  pl.BlockSpec((k, tn), lambda j, i: (0, j))],
        out_specs=pl.BlockSpec((tm, tn), lambda j, i: (i, j)),
        compiler_params=_params("arbitrary", "arbitrary"),
        name="matmul",
    )(x, w)


def _prep_kernel(*refs, has_vres):
    if has_vres:
        (rkv_ref, lora_ref, vd_ref, vf_ref, mu_rkv_ref, mu_lora_ref, mu_vd_ref, vec_ref,
         ww_ref, wa_ref, wg_ref, wv_ref, bo_ref,
         r_o, lw_o, k_o, v_o, a_o, b_o, g_o, c_rkv, c_lora, c_vd) = refs
    else:
        (rkv_ref, lora_ref, mu_rkv_ref, mu_lora_ref, vec_ref,
         ww_ref, wa_ref, wg_ref, bo_ref,
         r_o, lw_o, k_o, v_o, a_o, b_o, g_o, c_rkv, c_lora) = refs

    t = pl.program_id(1)
    carries = (c_rkv, c_lora) + ((c_vd,) if has_vres else ())

    @pl.when(t == 0)
    def _():
        for c in carries:
            c[...] = jnp.zeros_like(c)

    def shifted(x_ref, mu_ref, carry_ref):
        cur = x_ref[...]
        rows = cur.shape[0]
        rolled = pltpu.roll(cur, 1, 0)
        prev = jnp.where(_iota(cur.shape, 0) == 0, carry_ref[0:1, :], rolled)
        carry_ref[0:1, :] = cur[rows - 1:rows, :]
        return cur + mu_ref[...] * (prev - cur)

    xs = shifted(rkv_ref, mu_rkv_ref, c_rkv)
    xl = shifted(lora_ref, mu_lora_ref, c_lora)
    r = xs[:, :MIX_WIDTH]
    k = xs[:, MIX_WIDTH:2 * MIX_WIDTH]
    v = xs[:, 2 * MIX_WIDTH:]
    wa_in = xl[:, :LANES]
    gd = xl[:, LANES:]
    w0, a0, k_k, k_a = (vec_ref[i:i + 1, :] for i in range(4))

    log_w = -_softplus(-(w0 + _dot(jnp.tanh(wa_in), ww_ref[...]))) - 0.5
    lw_o[...] = -jnp.exp(log_w)
    a_lr = _sigmoid(a0 + _dot(wa_in, wa_ref[...]))
    if has_vres:
        xv = shifted(vd_ref, mu_vd_ref, c_vd)
        v0 = vec_ref[4:5, :]
        v = v + (vf_ref[...] - v) * _sigmoid(v0 + _dot(xv, wv_ref[...]))
    g_o[...] = _dot(_sigmoid(gd), wg_ref[...])
    kk = k * k_k
    norm = jnp.sqrt(_dot_sel(kk * kk, bo_ref[...]))
    kk = kk / jnp.maximum(norm, 1e-12)
    r_o[...] = r
    k_o[...] = k * (1.0 + (a_lr - 1.0) * k_a)
    v_o[...] = v
    a_o[...] = -kk
    b_o[...] = kk * a_lr


def _rwkv_prep(proj, vd, v_first, mu_rkv, mu_lora, mu_vd, vec, ww, wa, wg, wv, bo, batch, seq):
    has_vres = vd is not None
    m = proj.shape[0]
    tt = min(256, seq)
    nt = seq // tt
    rkv_w = 3 * MIX_WIDTH
    row = lambda b, t: (b * nt + t, 0)
    const = lambda b, t: (0, 0)
    full = lambda a: pl.BlockSpec(a.shape, const)
    in_specs = [pl.BlockSpec((tt, rkv_w), row),
                pl.BlockSpec((tt, 2 * LANES), lambda b, t: (b * nt + t, rkv_w // (2 * LANES)))]
    args = [proj, proj]
    if has_vres:
        in_specs += [pl.BlockSpec((tt, LANES), row), pl.BlockSpec((tt, MIX_WIDTH), row)]
        args += [vd, v_first]
    in_specs += [full(mu_rkv), full(mu_lora)]
    args += [mu_rkv, mu_lora]
    if has_vres:
        in_specs.append(full(mu_vd))
        args.append(mu_vd)
    in_specs += [full(vec), full(ww), full(wa), full(wg)]
    args += [vec, ww, wa, wg]
    if has_vres:
        in_specs.append(full(wv))
        args.append(wv)
    in_specs.append(full(bo))
    args.append(bo)
    scratch = [pltpu.VMEM((8, rkv_w), F32), pltpu.VMEM((8, 2 * LANES), F32)]
    if has_vres:
        scratch.append(pltpu.VMEM((8, LANES), F32))
    out = jax.ShapeDtypeStruct((m, MIX_WIDTH), F32)
    return pl.pallas_call(
        functools.partial(_prep_kernel, has_vres=has_vres),
        out_shape=(out,) * 7,
        grid=(batch, nt),
        in_specs=in_specs,
        out_specs=(pl.BlockSpec((tt, MIX_WIDTH), row),) * 7,
        scratch_shapes=scratch,
        compiler_params=_params("arbitrary", "arbitrary"),
        name="rwkv_prep",
    )(*args)


def _wkv_kernel(r_ref, lw_ref, k_ref, v_ref, a_ref, b_ref, g_ref, vec_ref, o_ref, h_ref):
    c = pl.program_id(1)
    chunk = r_ref.shape[0]
    two = 2 * chunk

    @pl.when(c == 0)
    def _():
        h_ref[...] = jnp.zeros_like(h_ref)

    tri = (_iota((chunk, chunk), 1) <= _iota((chunk, chunk), 0)).astype(BF16)
    log_p_all = _sel_dot(tri, lw_ref[...])
    head0 = _iota((1, LANES), 1) < HEAD_DIM
    rr, cc = _iota((two, two), 0), _iota((two, two), 1)
    strict, incl = cc < rr, cc <= rr
    block_ones = ((rr // HEAD_DIM) == (cc // HEAD_DIM)).astype(BF16)
    lnx_g, lnx_b, r_k = (vec_ref[i:i + 1, :] for i in range(3))

    def stack(x):
        return jnp.concatenate([jnp.where(head0, x, 0.0), jnp.where(head0, 0.0, x)], axis=0)

    pairs = range(r_ref.shape[1] // LANES)
    sls = [slice(p * LANES, (p + 1) * LANES) for p in pairs]
    lhs, rhs, bk, v2, decay_end, bonus = [], [], [], [], [], []
    for sl in sls:
        lw, log_p = lw_ref[:, sl], log_p_all[:, sl]
        log_p_end = log_p[chunk - 1:chunk, :]
        inv_p = jnp.exp(-log_p)
        to_end = jnp.exp(log_p_end - log_p)
        r, k, v, a, b = r_ref[:, sl], k_ref[:, sl], v_ref[:, sl], a_ref[:, sl], b_ref[:, sl]
        lhs.append(jnp.concatenate([stack(a * jnp.exp(log_p - lw)), stack(r * jnp.exp(log_p))], axis=0))
        rhs.append(jnp.concatenate([stack(b * inv_p), stack(k * inv_p)], axis=0))
        bk.append(jnp.concatenate([stack(b * to_end), stack(k * to_end)], axis=0))
        v2.append(stack(v))
        decay_end.append(jnp.exp(log_p_end))
        bonus.append((r * k * r_k[:, sl], v))
    gram = [_dot_nt(lhs[p], rhs[p]) for p in pairs]
    h_t = [h_ref[p] for p in pairs]
    from_state = [_dot_nt(lhs[p], h_t[p]) for p in pairs]
    a_pow = [jnp.where(strict, gram[p][:two, :two], 0.0) for p in pairs]
    u2 = [from_state[p][:two] + _dot(jnp.where(strict, gram[p][:two, two:], 0.0), v2[p]) for p in pairs]
    span = 1
    while span < chunk:
        u2 = [u2[p] + _dot(a_pow[p], u2[p]) for p in pairs]
        span *= 2
        if span < chunk:
            a_pow = [_dot(a_pow[p], a_pow[p]) for p in pairs]
    y2 = [from_state[p][two:] + _dot(jnp.where(incl, gram[p][two:, :two], 0.0), u2[p])
          + _dot(jnp.where(incl, gram[p][two:, two:], 0.0), v2[p]) for p in pairs]
    for p in pairs:
        h_ref[p] = h_t[p] * decay_end[p] + _dot_tn(jnp.concatenate([u2[p], v2[p]], axis=0), bk[p])

    y = [y2[p][:chunk] + y2[p][chunk:] for p in pairs]
    mean = [_dot_sel(y[p], block_ones) * (1.0 / HEAD_DIM) for p in pairs]
    d = [y[p] - mean[p] for p in pairs]
    var = [_dot_sel(d[p] * d[p], block_ones) * (1.0 / HEAD_DIM) for p in pairs]
    rk = [_dot_sel(bonus[p][0], block_ones) for p in pairs]
    for p, sl in zip(pairs, sls):
        yn = d[p] * lax.rsqrt(var[p] + GN_EPS) * lnx_g[:, sl] + lnx_b[:, sl]
        o_ref[:, sl] = ((yn + rk[p] * bonus[p][1]) * g_ref[:, sl]).astype(o_ref.dtype)


def _wkv_scan(r, lw, k, v, a, b, g, vec, batch, seq):
    m = r.shape[0]
    assert 2 * WKV_CHUNK == LANES and seq % WKV_CHUNK == 0
    nc = seq // WKV_CHUNK
    row = lambda bi, c: (bi * nc + c, 0)
    blk = pl.BlockSpec((WKV_CHUNK, MIX_WIDTH), row)
    return pl.pallas_call(
        _wkv_kernel,
        out_shape=jax.ShapeDtypeStruct((m, MIX_WIDTH), BF16),
        grid=(batch, nc),
        in_specs=[blk] * 7 + [pl.BlockSpec(vec.shape, lambda bi, c: (0, 0))],
        out_specs=blk,
        scratch_shapes=[pltpu.VMEM((MIX_WIDTH // LANES, LANES, LANES), F32)],
        compiler_params=_params("arbitrary", "arbitrary"),
        name="wkv_scan",
    )(r, lw, k, v, a, b, g, vec)


def _mem_attn_kernel(q_ref, mk_ref, mv_ref, o_ref):
    q, mk, mv = q_ref[...], mk_ref[...], mv_ref[...]
    head_of_lane = _iota((1, MEM_WIDTH), 1) // HEAD_DIM
    acc = jnp.zeros(q.shape, F32)
    for h in range(MEM_WIDTH // HEAD_DIM):
        mine = head_of_lane == h
        s = _dot_nt(q, jnp.where(mine, mk, 0.0)) * (HEAD_DIM ** -0.5)
        s = s - jnp.max(s, axis=-1, keepdims=True)
        e = jnp.exp(s)
        p = e / jnp.sum(e, axis=-1, keepdims=True)
        acc = acc + _dot(p, jnp.where(mine, mv, 0.0))
    o_ref[...] = acc.astype(o_ref.dtype)


def _mem_attention(proj, q_col_block, mkv, batch, seq, mem_tokens):
    m = proj.shape[0]
    tq = min(512, seq)
    nq = seq // tq
    return pl.pallas_call(
        _mem_attn_kernel,
        out_shape=jax.ShapeDtypeStruct((m, MEM_WIDTH), BF16),
        grid=(batch, nq),
        in_specs=[pl.BlockSpec((tq, MEM_WIDTH), lambda b, i: (b * nq + i, q_col_block)),
                  pl.BlockSpec((mem_tokens, MEM_WIDTH), lambda b, i: (b, 0)),
                  pl.BlockSpec((mem_tokens, MEM_WIDTH), lambda b, i: (b, 1))],
        out_specs=pl.BlockSpec((tq, MEM_WIDTH), lambda b, i: (b * nq + i, 0)),
        compiler_params=_params("arbitrary", "arbitrary"),
        name="mem_attention",
    )(proj, mkv, mkv)


def _stick_kernel(q_ref, k_ref, v_ref, o_ref, kt0, kt1, v0, v1, later_ref):
    i = pl.program_id(2)
    blk = q_ref.shape[0]
    seq = k_ref.shape[0]
    rr, cc = _iota((blk, blk), 0), _iota((blk, blk), 1)

    @pl.when(i == 0)
    def _():
        head0_lane = _iota((1, LANES), 1) < HEAD_DIM
        head0_row = _iota((LANES, 1), 0) < HEAD_DIM
        step = min(seq, 512)
        for c in range(seq // step):
            rows = slice(c * step, (c + 1) * step)
            kt = k_ref[rows, :].T
            kt0[:, rows] = jnp.where(head0_row, kt, 0.0).astype(BF16)
            kt1[:, rows] = jnp.where(head0_row, 0.0, kt).astype(BF16)
            vf = v_ref[rows, :]
            v0[rows, :] = jnp.where(head0_lane, vf, 0.0).astype(BF16)
            v1[rows, :] = jnp.where(head0_lane, 0.0, vf).astype(BF16)
        later_ref[...] = (rr > cc).astype(BF16)

    q = (q_ref[...] * (HEAD_DIM ** -0.5)).astype(BF16)
    causal = cc < rr

    def blocks(js, state, mask):
        acc, carries = state[0], list(state[1:])
        chains = [(j, h) for j in range(len(js)) for h in (0, 1)]
        keys = [pl.ds(pl.multiple_of(j * blk, blk), blk) for j in js]
        later = later_ref[...]
        z, sp, within = {}, {}, {}
        for n in range(len(chains) + 2):
            if n < len(chains):
                j, h = chains[n]
                z[n] = _dot(q, (kt0, kt1)[h][:, keys[j]]) * LOG2E
                minus_abs = lax.bitcast_convert_type(lax.bitcast_convert_type(z[n], jnp.int32) | SIGN_BIT, F32)
                s = jnp.maximum(z[n], 0.0) + jnp.log2(1.0 + jnp.exp2(minus_abs))
                sp[n] = s if mask is None else jnp.where(mask, s, 0.0)
            if 0 <= n - 1 < len(chains):
                within[n - 1] = _dot(sp[n - 1], later)
            if 0 <= n - 2:
                j, h = chains[n - 2]
                w = jnp.exp2(z[n - 2] - sp[n - 2] - within[n - 2] - carries[h])
                if mask is not None:
                    w = jnp.where(mask, w, 0.0)
                acc = acc + _dot(w, (v0, v1)[h][keys[j], :])
                carries[h] = carries[h] + jnp.sum(sp[n - 2], axis=-1, keepdims=True)
        return (acc, *carries)

    zero = jnp.zeros((blk, 1), F32)
    state = blocks([i], (jnp.zeros((blk, LANES), F32), zero, zero), causal)
    state = lax.fori_loop(0, i // 4, lambda n, s: blocks([i - 1 - 4 * n - c for c in range(4)], s, None), state)
    state = lax.fori_loop(0, (i % 4) // 2, lambda n, s: blocks([i % 4 - 1, i % 4 - 2], s, None), state)
    state = lax.fori_loop(0, i % 2, lambda n, s: blocks([0], s, None), state)
    o_ref[...] = state[0].astype(o_ref.dtype)


def _stick_attention(proj, kv, batch, seq):
    m = proj.shape[0]
    blk = min(ATT_BLOCK, seq)
    nq = seq // blk
    n_pairs = MIX_WIDTH // LANES
    return pl.pallas_call(
        _stick_kernel,
        out_shape=jax.ShapeDtypeStruct((m, MIX_WIDTH), BF16),
        grid=(batch, n_pairs, nq),
        in_specs=[pl.BlockSpec((blk, LANES), lambda b, p, i: (b * nq + i, p)),
                  pl.BlockSpec((seq, LANES), lambda b, p, i: (b, p)),
                  pl.BlockSpec((seq, LANES), lambda b, p, i: (b, n_pairs + p))],
        out_specs=pl.BlockSpec((blk, LANES), lambda b, p, i: (b * nq + i, p)),
        scratch_shapes=[pltpu.VMEM((LANES, seq), BF16)] * 2 + [pltpu.VMEM((seq, LANES), BF16)] * 2
                       + [pltpu.VMEM((blk, blk), BF16)],
        compiler_params=_params("arbitrary", "arbitrary", "arbitrary"),
        name="stick_attention",
    )(proj, kv, kv)


def _mix_ln_kernel(tok_ref, mo_ref, x_ref, wa_ref, wb_ref, ln_ref, o_ref, ob_ref, *, alpha):
    mix = _dot(tok_ref[...], wa_ref[...]) + _dot(mo_ref[...], wb_ref[...])
    y = _layer_norm(alpha * x_ref[...] + mix, ln_ref[0:1, :], ln_ref[1:2, :])
    o_ref[...] = y
    ob_ref[...] = y.astype(BF16)


def _mix_ln(tok, mo, x, w_o, ln, alpha):
    m, d = x.shape
    tm = min(512, m)
    row = lambda i: (i, 0)
    return pl.pallas_call(
        functools.partial(_mix_ln_kernel, alpha=alpha),
        out_shape=(jax.ShapeDtypeStruct((m, d), F32), jax.ShapeDtypeStruct((m, d), BF16)),
        grid=(m // tm,),
        in_specs=[pl.BlockSpec((tm, MIX_WIDTH), row), pl.BlockSpec((tm, MEM_WIDTH), row),
                  pl.BlockSpec((tm, d), row),
                  pl.BlockSpec((MIX_WIDTH, d), lambda i: (0, 0)),
                  pl.BlockSpec((MEM_WIDTH, d), lambda i: (MIX_WIDTH // MEM_WIDTH, 0)),
                  pl.BlockSpec(ln.shape, lambda i: (0, 0))],
        out_specs=(pl.BlockSpec((tm, d), row),) * 2,
        compiler_params=_params("arbitrary"),
        name="mix_ln",
    )(tok, mo, x, w_o, w_o, ln)


HI16 = -65536


def _pack_bf16_pairs(x):
    half = x.shape[1] // 2
    xb = x.astype(BF16).astype(F32)
    lo = lax.bitcast_convert_type(xb[:, :half], jnp.int32)
    hi = lax.bitcast_convert_type(xb[:, half:], jnp.int32)
    return (hi & HI16) | lax.shift_right_logical(lo, 16)


def _unpack_bf16_pairs(u):
    lo = lax.bitcast_convert_type(lax.shift_left(u, 16), F32)
    hi = lax.bitcast_convert_type(u & HI16, F32)
    return jnp.concatenate([lo, hi], axis=-1).astype(BF16)


def _router_kernel(x_ref, w_ref, comb_ref, info_ref, xpk_ref, cnt_ref, carry_ref, *, n_experts):
    @pl.when(pl.program_id(0) == 0)
    def _():
        carry_ref[...] = jnp.zeros_like(carry_ref)

    x = x_ref[...]
    logits = jnp.dot(x, w_ref[...], preferred_element_type=F32, precision=lax.Precision.HIGHEST)
    lane = _iota(logits.shape, 1).astype(F32)
    neg = jnp.float32(-jnp.inf)
    logits = jnp.where(lane < n_experts, logits, neg)
    m1 = jnp.max(logits, axis=-1, keepdims=True)
    i1 = jnp.min(jnp.where(logits == m1, lane, float(LANES)), axis=-1, keepdims=True)
    rest = jnp.where(lane == i1, neg, logits)
    m2 = jnp.max(rest, axis=-1, keepdims=True)
    i2 = jnp.min(jnp.where(rest == m2, lane, float(LANES)), axis=-1, keepdims=True)
    e2 = jnp.exp(m2 - m1)
    w1 = 1.0 / (1.0 + e2)
    w2 = e2 / (1.0 + e2)
    first, second = lane == i1, lane == i2
    comb_ref[...] = jnp.where(first, w1, 0.0) + jnp.where(second, w2, 0.0)

    sel = jnp.where(first, 1.0, 0.0) + jnp.where(second, 1.0, 0.0)
    rows = sel.shape[0]
    earlier = (_iota((rows, rows), 1) < _iota((rows, rows), 0)).astype(BF16)
    rank = _dot(earlier, sel) + carry_ref[0:1, :]
    total = carry_ref[0:1, :] + jnp.sum(sel, axis=0, keepdims=True)
    carry_ref[0:1, :] = total
    cnt_ref[...] = jnp.broadcast_to(total, cnt_ref.shape)
    rank1 = jnp.sum(jnp.where(first, rank, 0.0), axis=-1, keepdims=True)
    rank2 = jnp.sum(jnp.where(second, rank, 0.0), axis=-1, keepdims=True)
    info_ref[...] = jnp.where(lane == 0, i1, jnp.where(lane == 1, i2, jnp.where(lane == 2, rank1,
                              jnp.where(lane == 3, rank2, 0.0))))
    xpk_ref[...] = _pack_bf16_pairs(x)


def _router(x, w_router):
    m, d = x.shape
    n_experts = w_router.shape[1]
    w_pad = jnp.pad(w_router, ((0, 0), (0, LANES - n_experts)))
    tm = min(512, m)
    row = lambda i: (i, 0)
    return pl.pallas_call(
        functools.partial(_router_kernel, n_experts=n_experts),
        out_shape=(jax.ShapeDtypeStruct((m, LANES), F32), jax.ShapeDtypeStruct((m, LANES), F32),
                   jax.ShapeDtypeStruct((m, d // 2), jnp.int32), jax.ShapeDtypeStruct((8, LANES), F32)),
        grid=(m // tm,),
        in_specs=[pl.BlockSpec((tm, d), row), pl.BlockSpec((d, LANES), lambda i: (0, 0))],
        out_specs=(pl.BlockSpec((tm, LANES), row), pl.BlockSpec((tm, LANES), row),
                   pl.BlockSpec((tm, d // 2), row), pl.BlockSpec((8, LANES), lambda i: (0, 0))),
        scratch_shapes=[pltpu.VMEM((8, LANES), F32)],
        compiler_params=_params("arbitrary"),
        name="router",
    )(x, w_pad)


def _moe_src_kernel(pos_ref, src_ref):
    def clear(i, c):
        src_ref[i] = 0
        return c

    lax.fori_loop(0, src_ref.shape[0], clear, 0, unroll=16)

    def place(t, c):
        src_ref[pos_ref[2 * t]] = t
        src_ref[pos_ref[2 * t + 1]] = t
        return c

    lax.fori_loop(0, pos_ref.shape[0] // 2, place, 0, unroll=8)


def _moe_src(pos, n_rows):
    smem = pl.BlockSpec(memory_space=pltpu.SMEM)
    return pl.pallas_call(
        _moe_src_kernel,
        out_shape=jax.ShapeDtypeStruct((n_rows,), jnp.int32),
        in_specs=[smem],
        out_specs=smem,
        name="moe_src",
    )(pos)


def _moe_dispatch_kernel(src_ref, xpk_ref, comb_ref, xs_ref, ws_ref):
    rows = xs_ref.shape[0]
    base = pl.program_id(0) * rows

    def copy(i, c):
        t = src_ref[base + i]
        xs_ref[pl.ds(i, 1), :] = xpk_ref[pl.ds(t, 1), :]
        ws_ref[pl.ds(i, 1), :] = comb_ref[pl.ds(t, 1), :]
        return c

    lax.fori_loop(0, rows, copy, 0, unroll=8)


def _moe_dispatch(src, xpk, comb, n_tiles):
    m, half = xpk.shape
    whole = lambda a: pl.BlockSpec(a.shape, lambda r, s: (0, 0), pipeline_mode=pl.Buffered(1))
    tile = lambda w: pl.BlockSpec((MOE_TILE, w), lambda r, s: (r, 0))
    return pl.pallas_call(
        _moe_dispatch_kernel,
        out_shape=(jax.ShapeDtypeStruct((n_tiles * MOE_TILE, half), jnp.int32),
                   jax.ShapeDtypeStruct((n_tiles * MOE_TILE, LANES), F32)),
        grid_spec=pltpu.PrefetchScalarGridSpec(
            num_scalar_prefetch=1, grid=(n_tiles,),
            in_specs=[whole(xpk), whole(comb)],
            out_specs=(tile(half), tile(LANES))),
        compiler_params=_params("arbitrary"),
        name="moe_dispatch",
    )(src, xpk, comb)


def _moe_up_kernel(te_ref, tv_ref, xs_ref, ws_ref, wg_ref, wu_ref, o_ref):
    r = pl.program_id(1)

    @pl.when(tv_ref[r] == 1)
    def _():
        x = _unpack_bf16_pairs(xs_ref[...])
        gate = _dot(x, wg_ref[0])
        h = gate * _sigmoid(gate) * _dot(x, wu_ref[0])
        ws = ws_ref[...]
        mine = _iota(ws.shape, 1) == te_ref[r]
        o_ref[...] = (h * jnp.sum(jnp.where(mine, ws, 0.0), axis=-1, keepdims=True)).astype(o_ref.dtype)

    @pl.when(tv_ref[r] == 0)
    def _():
        o_ref[...] = jnp.zeros_like(o_ref)


def _moe_up(tile_expert, tile_valid, xs, ws, w_gate_up, w_base):
    n_rows, half = xs.shape
    d = 2 * half
    d_ff = w_gate_up.shape[2] // 2
    n_split = 2
    tn = d_ff // n_split
    return pl.pallas_call(
        _moe_up_kernel,
        out_shape=jax.ShapeDtypeStruct((n_rows, d_ff), BF16),
        grid_spec=pltpu.PrefetchScalarGridSpec(
            num_scalar_prefetch=2, grid=(n_split, n_rows // MOE_TILE),
            in_specs=[pl.BlockSpec((MOE_TILE, half), lambda n, r, te, tv: (r, 0)),
                      pl.BlockSpec((MOE_TILE, LANES), lambda n, r, te, tv: (r, 0)),
                      pl.BlockSpec((1, d, tn), lambda n, r, te, tv: (w_base + te[r], 0, n)),
                      pl.BlockSpec((1, d, tn), lambda n, r, te, tv: (w_base + te[r], 0, n + n_split))],
            out_specs=pl.BlockSpec((MOE_TILE, tn), lambda n, r, te, tv: (r, n))),
        compiler_params=_params("arbitrary", "arbitrary"),
        name="moe_up",
    )(tile_expert, tile_valid, xs, ws, w_gate_up, w_gate_up)


def _moe_down_kernel(te_ref, tv_ref, h_ref, wd_ref, o_ref):
    r = pl.program_id(0)

    @pl.when(tv_ref[r] == 1)
    def _():
        o_ref[...] = _dot(h_ref[...], wd_ref[0]).astype(o_ref.dtype)

    @pl.when(tv_ref[r] == 0)
    def _():
        o_ref[...] = jnp.zeros_like(o_ref)


def _moe_down(tile_expert, tile_valid, h, w_down, w_base):
    n_rows, d_ff = h.shape
    d = w_down.shape[2]
    return pl.pallas_call(
        _moe_down_kernel,
        out_shape=jax.ShapeDtypeStruct((n_rows, d), BF16),
        grid_spec=pltpu.PrefetchScalarGridSpec(
            num_scalar_prefetch=2, grid=(n_rows // MOE_TILE,),
            in_specs=[pl.BlockSpec((MOE_TILE, d_ff), lambda r, te, tv: (r, 0)),
                      pl.BlockSpec((1, d_ff, d), lambda r, te, tv: (w_base + te[r], 0, 0))],
            out_specs=pl.BlockSpec((MOE_TILE, d), lambda r, te, tv: (r, 0))),
        compiler_params=_params("arbitrary"),
        name="moe_down",
    )(tile_expert, tile_valid, h, w_down)


def _moe_combine_kernel(win_ref, ok_ref, rs_ref, info_ref, x_ref, ln_ref, *refs, alpha, n_experts):
    n_win = 2 * n_experts
    windows, (o_ref, ob_ref) = refs[:n_win], refs[n_win:]
    i = pl.program_id(0)
    info = info_ref[...]
    i1, i2, rank1, rank2 = (info[:, c:c + 1] for c in range(4))
    pos1, pos2 = rank1, rank2
    for e in range(n_experts):
        start = rs_ref[e].astype(F32)
        pos1 = pos1 + jnp.where(i1 == e, start, 0.0)
        pos2 = pos2 + jnp.where(i2 == e, start, 0.0)
    rows = windows[0].shape[0]
    lane = _iota((1, rows), 1).astype(F32)
    acc = None
    for k in range(n_win):
        base = jnp.where(ok_ref[i * n_win + k] == 1, win_ref[i * n_win + k] * rows, -rows).astype(F32)
        pick = jnp.where(pos1 - base == lane, 1.0, jnp.where(pos2 - base == lane, 1.0, 0.0))
        part = _dot(pick, windows[k][...])
        acc = part if acc is None else acc + part
    y = _layer_norm(alpha * x_ref[...] + acc, ln_ref[0:1, :], ln_ref[1:2, :])
    o_ref[...] = y
    ob_ref[...] = y.astype(BF16)


def _moe_combine(win, ok, row_start, info, x, ln, out, alpha, n_experts):
    m, d = x.shape
    tt = COMBINE_TILE
    n_win = 2 * n_experts
    row = lambda i, *_: (i, 0)
    window = lambda k: pl.BlockSpec((tt, d), lambda i, w, o, s: (w[i * n_win + k], 0))
    return pl.pallas_call(
        functools.partial(_moe_combine_kernel, alpha=alpha, n_experts=n_experts),
        out_shape=(jax.ShapeDtypeStruct((m, d), F32), jax.ShapeDtypeStruct((m, d), BF16)),
        grid_spec=pltpu.PrefetchScalarGridSpec(
            num_scalar_prefetch=3, grid=(m // tt,),
            in_specs=[pl.BlockSpec((tt, LANES), row), pl.BlockSpec((tt, d), row),
                      pl.BlockSpec(ln.shape, lambda i, *_: (0, 0))] + [window(k) for k in range(n_win)],
            out_specs=(pl.BlockSpec((tt, d), row),) * 2),
        compiler_params=_params("arbitrary"),
        name="moe_combine",
    )(win, ok, row_start, info, x, ln, *([out] * n_win))


def _moe_swiglu(x, w_router, w_gate_up, w_down, w_base, ln, alpha):
    m, d = x.shape
    n_experts = w_router.shape[1]
    assert (TOP_K * m) % MOE_TILE == 0 and m % COMBINE_TILE == 0 and MOE_TILE % COMBINE_TILE == 0
    n_tiles = TOP_K * m // MOE_TILE + n_experts
    comb, info, xpk, counts = _router(x, w_router)

    i32 = jnp.int32
    experts = jnp.arange(n_experts, dtype=i32)
    counts = counts[0, :n_experts].astype(i32)
    tiles = (counts + MOE_TILE - 1) // MOE_TILE
    tile_end = jnp.cumsum(tiles)
    row_start = (tile_end - tiles) * MOE_TILE
    t = jnp.arange(n_tiles, dtype=i32)
    tile_expert = jnp.minimum(jnp.sum(t[:, None] >= tile_end[None, :], axis=1), n_experts - 1).astype(i32)
    tile_valid = (t < tile_end[-1]).astype(i32)
    i1, i2 = info[:, 0].astype(i32), info[:, 1].astype(i32)
    hit1, hit2 = i1[:, None] == experts[None, :], i2[:, None] == experts[None, :]
    pos1 = info[:, 2].astype(i32) + jnp.sum(jnp.where(hit1, row_start[None, :], 0), axis=1)
    pos2 = info[:, 3].astype(i32) + jnp.sum(jnp.where(hit2, row_start[None, :], 0), axis=1)
    per_tile = jnp.sum((hit1 | hit2).astype(i32).reshape(m // COMBINE_TILE, COMBINE_TILE, n_experts), axis=1)
    first = row_start[None, :] + jnp.cumsum(per_tile, axis=0) - per_tile
    win0 = first // COMBINE_TILE
    win1 = (first + jnp.maximum(per_tile, 1) - 1) // COMBINE_TILE
    ok0 = per_tile > 0
    ok1 = ok0 & (win1 != win0)
    win = jnp.stack([jnp.where(ok0, win0, 0), jnp.where(ok1, win1, 0)], axis=-1).reshape(-1).astype(i32)
    ok = jnp.stack([ok0, ok1], axis=-1).reshape(-1).astype(i32)

    src = _moe_src(jnp.stack([pos1, pos2], axis=-1).reshape(-1), n_tiles * MOE_TILE)
    xs, ws = _moe_dispatch(src, xpk, comb, n_tiles)
    h = _moe_up(tile_expert, tile_valid, xs, ws, w_gate_up, w_base)
    out = _moe_down(tile_expert, tile_valid, h, w_down, w_base)
    return _moe_combine(win, ok, row_start, info, x, ln, out, alpha, n_experts)


def _ffn_up_kernel(x_ref, wg_ref, wu_ref, o_ref):
    x = x_ref[...]
    gate = _dot(x, wg_ref[0])
    o_ref[0] = (gate * _sigmoid(gate) * _dot(x, wu_ref[0])).astype(o_ref.dtype)


def _ffn_up(xb, w_gate_up):
    m, d = xb.shape
    n_e = w_gate_up.shape[0]
    d_ff = w_gate_up.shape[2] // 2
    n_split = 2
    tn = d_ff // n_split
    tm = min(512, m)
    return pl.pallas_call(
        _ffn_up_kernel,
        out_shape=jax.ShapeDtypeStruct((n_e, m, d_ff), BF16),
        grid=(n_e, n_split, m // tm),
        in_specs=[pl.BlockSpec((tm, d), lambda e, n, i: (i, 0)),
                  pl.BlockSpec((1, d, tn), lambda e, n, i: (e, 0, n)),
                  pl.BlockSpec((1, d, tn), lambda e, n, i: (e, 0, n + n_split))],
        out_specs=pl.BlockSpec((1, tm, tn), lambda e, n, i: (e, i, n)),
        compiler_params=_params("arbitrary", "arbitrary", "arbitrary"),
        name="ffn_up",
    )(xb, w_gate_up, w_gate_up)


def _ffn_down_kernel(h_ref, wd_ref, x_ref, ln_ref, o_ref, ob_ref, acc_ref, *, alpha):
    e = pl.program_id(1)

    @pl.when(e == 0)
    def _():
        acc_ref[...] = jnp.zeros_like(acc_ref)

    acc_ref[...] += _dot(h_ref[0], wd_ref[0])

    @pl.when(e == pl.num_programs(1) - 1)
    def _():
        y = _layer_norm(alpha * x_ref[...] + acc_ref[...], ln_ref[0:1, :], ln_ref[1:2, :])
        o_ref[...] = y
        ob_ref[...] = y.astype(BF16)


def _ffn_down(h, w_down, x, ln, alpha):
    n_e, m, d_ff = h.shape
    d = x.shape[1]
    tm = min(512, m)
    row = lambda i, e: (i, 0)
    return pl.pallas_call(
        functools.partial(_ffn_down_kernel, alpha=alpha),
        out_shape=(jax.ShapeDtypeStruct((m, d), F32), jax.ShapeDtypeStruct((m, d), BF16)),
        grid=(m // tm, n_e),
        in_specs=[pl.BlockSpec((1, tm, d_ff), lambda i, e: (e, i, 0)),
                  pl.BlockSpec((1, d_ff, d), lambda i, e: (e, 0, 0)),
                  pl.BlockSpec((tm, d), row),
                  pl.BlockSpec(ln.shape, lambda i, e: (0, 0))],
        out_specs=(pl.BlockSpec((tm, d), row),) * 2,
        scratch_shapes=[pltpu.VMEM((tm, d), F32)],
        compiler_params=_params("arbitrary", "arbitrary"),
        name="ffn_down",
    )(h, w_down, x, ln)


def _pad_rows(w, rows_before, rows_total):
    return jnp.pad(w, ((rows_before, rows_total - rows_before - w.shape[0]), (0, 0)))


def kernel(x, mem, a_w_in_first, a_w_in_rest, a_mu_first, a_mu_rest, a_vec, a_w_up, a_a_up, a_g_up, a_v0, a_v_up, a_r_k, b_w_in, w_kv_shared, mem_kv, w_o, ln, ffn_gate_up, ffn_down, router, exp_gate_up, exp_down):
    batch, seq, d_model = x.shape
    mem_tokens = mem.shape[1]
    depth = w_o.shape[0]
    n_a = a_vec.shape[0]
    alpha = (2 * depth) ** 0.25
    m = batch * seq
    rkv_w = 3 * MIX_WIDTH
    lora_w = DECAY_LORA + ICLR_LORA + GATE_LORA
    assert DECAY_LORA + ICLR_LORA == LANES and lora_w == 2 * LANES and rkv_w % lora_w == 0

    xf = x.reshape(m, d_model)
    xb = xf.astype(BF16)
    memb = mem.reshape(batch * mem_tokens, d_model).astype(BF16)
    hh = jnp.arange(MIX_WIDTH) // HEAD_DIM
    block_ones = (hh[:, None] == hh[None, :]).astype(BF16)

    v_first = None
    kv = None
    for l in range(depth):
        mkv = _matmul(memb, mem_kv[l].astype(BF16), tn=2 * MEM_WIDTH)
        if l < n_a:
            w_in = a_w_in_first if l == 0 else a_w_in_rest[l - 1]
            mu = a_mu_first if l == 0 else a_mu_rest[l - 1]
            has_vres = l > 0
            n_cols = w_in.shape[1] - MEM_WIDTH
            gate_lo = n_cols - GATE_LORA
            order = jnp.concatenate([jnp.arange(rkv_w + DECAY_LORA + ICLR_LORA), jnp.arange(gate_lo, n_cols),
                                     jnp.arange(n_cols, n_cols + MEM_WIDTH)])
            proj = _matmul(xb, w_in[:, order].astype(BF16), tn=(rkv_w + lora_w + MEM_WIDTH) // 2)
            mu_rkv = mu[:rkv_w].reshape(1, rkv_w)
            mu_lora = jnp.concatenate([mu[rkv_w:rkv_w + LANES], mu[gate_lo:n_cols]]).reshape(1, lora_w)
            ww = _pad_rows(a_w_up[l], 0, LANES).astype(BF16)
            wa = _pad_rows(a_a_up[l], DECAY_LORA, LANES).astype(BF16)
            wg = a_g_up[l].astype(BF16)
            vec = jnp.concatenate([a_vec[l, :4], a_v0[l - 1][None] if has_vres else jnp.zeros((1, MIX_WIDTH), F32),
                                   jnp.zeros((3, MIX_WIDTH), F32)])
            if has_vres:
                vd_lo = rkv_w + LANES
                w_vd = jnp.pad(w_in[:, vd_lo:vd_lo + VALUE_LORA], ((0, 0), (0, LANES - VALUE_LORA)))
                vd = _matmul(xb, w_vd.astype(BF16), tn=LANES)
                mu_vd = jnp.pad(mu[vd_lo:vd_lo + VALUE_LORA], (0, LANES - VALUE_LORA)).reshape(1, LANES)
                wv = _pad_rows(a_v_up[l - 1], 0, LANES).astype(BF16)
                ops = _rwkv_prep(proj, vd, v_first, mu_rkv, mu_lora, mu_vd, vec, ww, wa, wg, wv, block_ones,
                                 batch, seq)
            else:
                ops = _rwkv_prep(proj, None, None, mu_rkv, mu_lora, None, vec, ww, wa, wg, None, block_ones,
                                 batch, seq)
                v_first = ops[3]
            scan_vec = jnp.concatenate([a_vec[l, 4:6], a_r_k[l].reshape(1, MIX_WIDTH),
                                        jnp.zeros((5, MIX_WIDTH), F32)])
            tok = _wkv_scan(*ops, scan_vec, batch, seq)
            q_col_block = (rkv_w + lora_w) // MEM_WIDTH
        else:
            proj = _matmul(xb, b_w_in[l - n_a].astype(BF16), tn=(MIX_WIDTH + MEM_WIDTH) // 2)
            tok = _stick_attention(proj, kv, batch, seq)
            q_col_block = MIX_WIDTH // MEM_WIDTH
        mo = _mem_attention(proj, q_col_block, mkv, batch, seq, mem_tokens)
        xf, xb = _mix_ln(tok, mo, xf, w_o[l].astype(BF16), ln[l, 0:2], alpha)
        if l % 2 == 0:
            h = _ffn_up(xb, ffn_gate_up[l // 2][None].astype(BF16))
            xf, xb = _ffn_down(h, ffn_down[l // 2][None].astype(BF16), xf, ln[l, 2:4], alpha)
        else:
            n_e = router.shape[2]
            xf, xb = _moe_swiglu(xf, router[l // 2], exp_gate_up.reshape((-1,) + exp_gate_up.shape[2:]),
                                 exp_down.reshape((-1,) + exp_down.shape[2:]), (l // 2) * n_e, ln[l, 2:4], alpha)
        if l == n_a - 1:
            kv = _matmul(xb, w_kv_shared.astype(BF16), tn=MIX_WIDTH)
    return xf.reshape(batch, seq, d_model)
```

```python
import functools

import jax
import jax.numpy as jnp
from jax import lax
from jax.experimental import pallas as pl
from jax.experimental.pallas import tpu as pltpu

F32 = jnp.float32
BF16 = jnp.bfloat16

HEAD_DIM = 64
LANES = 128
MIX_WIDTH = 768
MEM_WIDTH = 256
DECAY_LORA = 64
ICLR_LORA = 64
VALUE_LORA = 32
GATE_LORA = 128
TOP_K = 2
LN_EPS = 1e-5
GN_EPS = 64e-5
WKV_CHUNK = 64
ATT_BLOCK = 256
LOG2E = 1.4426950408889634
SIGN_BIT = -2147483648
MOE_TILE = 512
COMBINE_TILE = 256
VMEM_LIMIT = 56 * 1024 * 1024


def _params(*sem):
    return pltpu.CompilerParams(dimension_semantics=sem, vmem_limit_bytes=VMEM_LIMIT)


def _dot(a, b):
    return jnp.dot(a.astype(BF16), b.astype(BF16), preferred_element_type=F32)


def _dot_nt(a, b):
    return lax.dot_general(a.astype(BF16), b.astype(BF16), (((1,), (1,)), ((), ())),
                           preferred_element_type=F32)


def _dot_tn(a, b):
    return lax.dot_general(a.astype(BF16), b.astype(BF16), (((0,), (0,)), ((), ())),
                           preferred_element_type=F32)


def _split3(x):
    x1 = x.astype(BF16)
    r1 = x - x1.astype(F32)
    x2 = r1.astype(BF16)
    x3 = (r1 - x2.astype(F32)).astype(BF16)
    return x1, x2, x3


def _dot_sel(x, sel):
    x1 = x.astype(BF16)
    x2 = (x - x1.astype(F32)).astype(BF16)
    return _dot(x1, sel) + _dot(x2, sel)


def _sel_dot(sel, x):
    x1, x2, x3 = _split3(x)
    return _dot(sel, x1) + _dot(sel, x2) + _dot(sel, x3)


def _softplus(z):
    return jnp.maximum(z, 0.0) + jnp.log1p(jnp.exp(-jnp.abs(z)))


def _sigmoid(z):
    return 1.0 / (1.0 + jnp.exp(-z))


def _layer_norm(y, g, b):
    mu = jnp.mean(y, axis=-1, keepdims=True)
    d = y - mu
    var = jnp.mean(d * d, axis=-1, keepdims=True)
    return d * lax.rsqrt(var + LN_EPS) * g + b


def _iota(shape, dim):
    return lax.broadcasted_iota(jnp.int32, shape, dim)


def _mm_kernel(x_ref, w_ref, o_ref):
    o_ref[...] = _dot(x_ref[...], w_ref[...]).astype(o_ref.dtype)


def _matmul(x, w, tn, out_dtype=F32, tm=512):
    m, k = x.shape
    n = w.shape[1]
    tm = min(tm, m)
    return pl.pallas_call(
        _mm_kernel,
        out_shape=jax.ShapeDtypeStruct((m, n), out_dtype),
        grid=(n // tn, m // tm),
        in_specs=[pl.BlockSpec((tm, k), lambda j, i: (i, 0)),
                  pl.BlockSpec((k, tn), lambda j, i: (0, j))],
        out_specs=pl.BlockSpec((tm, tn), lambda j, i: (i, j)),
        compiler_params=_params("arbitrary", "arbitrary"),
        name="matmul",
    )(x, w)


def _prep_kernel(*refs, has_vres):
    if has_vres:
        (rkv_ref, lora_ref, vd_ref, vf_ref, mu_rkv_ref, mu_lora_ref, mu_vd_ref, vec_ref,
         ww_ref, wa_ref, wg_ref, wv_ref, bo_ref,
         r_o, lw_o, k_o, v_o, a_o, b_o, g_o, c_rkv, c_lora, c_vd) = refs
    else:
        (rkv_ref, lora_ref, mu_rkv_ref, mu_lora_ref, vec_ref,
         ww_ref, wa_ref, wg_ref, bo_ref,
         r_o, lw_o, k_o, v_o, a_o, b_o, g_o, c_rkv, c_lora) = refs

    t = pl.program_id(1)
    carries = (c_rkv, c_lora) + ((c_vd,) if has_vres else ())

    @pl.when(t == 0)
    def _():
        for c in carries:
            c[...] = jnp.zeros_like(c)

    def shifted(x_ref, mu_ref, carry_ref):
        cur = x_ref[...]
        rows = cur.shape[0]
        rolled = pltpu.roll(cur, 1, 0)
        prev = jnp.where(_iota(cur.shape, 0) == 0, carry_ref[0:1, :], rolled)
        carry_ref[0:1, :] = cur[rows - 1:rows, :]
        return cur + mu_ref[...] * (prev - cur)

    xs = shifted(rkv_ref, mu_rkv_ref, c_rkv)
    xl = shifted(lora_ref, mu_lora_ref, c_lora)
    r = xs[:, :MIX_WIDTH]
    k = xs[:, MIX_WIDTH:2 * MIX_WIDTH]
    v = xs[:, 2 * MIX_WIDTH:]
    wa_in = xl[:, :LANES]
    gd = xl[:, LANES:]
    w0, a0, k_k, k_a = (vec_ref[i:i + 1, :] for i in range(4))

    log_w = -_softplus(-(w0 + _dot(jnp.tanh(wa_in), ww_ref[...]))) - 0.5
    lw_o[...] = -jnp.exp(log_w)
    a_lr = _sigmoid(a0 + _dot(wa_in, wa_ref[...]))
    if has_vres:
        xv = shifted(vd_ref, mu_vd_ref, c_vd)
        v0 = vec_ref[4:5, :]
        v = v + (vf_ref[...] - v) * _sigmoid(v0 + _dot(xv, wv_ref[...]))
    g_o[...] = _dot(_sigmoid(gd), wg_ref[...])
    kk = k * k_k
    norm = jnp.sqrt(_dot_sel(kk * kk, bo_ref[...]))
    kk = kk / jnp.maximum(norm, 1e-12)
    r_o[...] = r
    k_o[...] = k * (1.0 + (a_lr - 1.0) * k_a)
    v_o[...] = v
    a_o[...] = -kk
    b_o[...] = kk * a_lr


def _rwkv_prep(proj, vd, v_first, mu_rkv, mu_lora, mu_vd, vec, ww, wa, wg, wv, bo, batch, seq):
    has_vres = vd is not None
    m = proj.shape[0]
    tt = min(256, seq)
    nt = seq // tt
    rkv_w = 3 * MIX_WIDTH
    row = lambda b, t: (b * nt + t, 0)
    const = lambda b, t: (0, 0)
    full = lambda a: pl.BlockSpec(a.shape, const)
    in_specs = [pl.BlockSpec((tt, rkv_w), row),
                pl.BlockSpec((tt, 2 * LANES), lambda b, t: (b * nt + t, rkv_w // (2 * LANES)))]
    args = [proj, proj]
    if has_vres:
        in_specs += [pl.BlockSpec((tt, LANES), row), pl.BlockSpec((tt, MIX_WIDTH), row)]
        args += [vd, v_first]
    in_specs += [full(mu_rkv), full(mu_lora)]
    args += [mu_rkv, mu_lora]
    if has_vres:
        in_specs.append(full(mu_vd))
        args.append(mu_vd)
    in_specs += [full(vec), full(ww), full(wa), full(wg)]
    args += [vec, ww, wa, wg]
    if has_vres:
        in_specs.append(full(wv))
        args.append(wv)
    in_specs.append(full(bo))
    args.append(bo)
    scratch = [pltpu.VMEM((8, rkv_w), F32), pltpu.VMEM((8, 2 * LANES), F32)]
    if has_vres:
        scratch.append(pltpu.VMEM((8, LANES), F32))
    out = jax.ShapeDtypeStruct((m, MIX_WIDTH), F32)
    return pl.pallas_call(
        functools.partial(_prep_kernel, has_vres=has_vres),
        out_shape=(out,) * 7,
        grid=(batch, nt),
        in_specs=in_specs,
        out_specs=(pl.BlockSpec((tt, MIX_WIDTH), row),) * 7,
        scratch_shapes=scratch,
        compiler_params=_params("arbitrary", "arbitrary"),
        name="rwkv_prep",
    )(*args)


def _wkv_kernel(r_ref, lw_ref, k_ref, v_ref, a_ref, b_ref, g_ref, vec_ref, o_ref, h_ref):
    c = pl.program_id(1)
    chunk = r_ref.shape[1]
    two = 2 * chunk

    @pl.when(c == 0)
    def _():
        h_ref[...] = jnp.zeros_like(h_ref)

    tri = (_iota((chunk, chunk), 1) <= _iota((chunk, chunk), 0)).astype(BF16)
    log_p_all = [_sel_dot(tri, lw_ref[bi]) for bi in range(r_ref.shape[0])]
    head0 = _iota((1, LANES), 1) < HEAD_DIM
    rr, cc = _iota((two, two), 0), _iota((two, two), 1)
    strict, incl = cc < rr, cc <= rr
    block_ones = ((rr // HEAD_DIM) == (cc // HEAD_DIM)).astype(BF16)
    lnx_g, lnx_b, r_k = (vec_ref[i:i + 1, :] for i in range(3))

    def stack(x):
        return jnp.concatenate([jnp.where(head0, x, 0.0), jnp.where(head0, 0.0, x)], axis=0)

    n_pairs = r_ref.shape[2] // LANES
    where = [(bi, slice(p * LANES, (p + 1) * LANES)) for bi in range(r_ref.shape[0]) for p in range(n_pairs)]
    pairs = range(len(where))
    lhs, rhs, bk, v2, decay_end, bonus = [], [], [], [], [], []
    for bi, sl in where:
        lw, log_p = lw_ref[bi, :, sl], log_p_all[bi][:, sl]
        log_p_end = log_p[chunk - 1:chunk, :]
        inv_p = jnp.exp(-log_p)
        to_end = jnp.exp(log_p_end - log_p)
        r, k, v, a, b = r_ref[bi, :, sl], k_ref[bi, :, sl], v_ref[bi, :, sl], a_ref[bi, :, sl], b_ref[bi, :, sl]
        lhs.append(jnp.concatenate([stack(a * jnp.exp(log_p - lw)), stack(r * jnp.exp(log_p))], axis=0))
        rhs.append(jnp.concatenate([stack(b * inv_p), stack(k * inv_p)], axis=0))
        bk.append(jnp.concatenate([stack(b * to_end), stack(k * to_end)], axis=0))
        v2.append(stack(v))
        decay_end.append(jnp.exp(log_p_end))
        bonus.append((r * k * r_k[:, sl], v))
    gram = [_dot_nt(lhs[p], rhs[p]) for p in pairs]
    h_t = [h_ref[p] for p in pairs]
    from_state = [_dot_nt(lhs[p], h_t[p]) for p in pairs]
    a_pow = [jnp.where(strict, gram[p][:two, :two], 0.0) for p in pairs]
    u2 = [from_state[p][:two] + _dot(jnp.where(strict, gram[p][:two, two:], 0.0), v2[p]) for p in pairs]
    span = 1
    while span < chunk:
        u2 = [u2[p] + _dot(a_pow[p], u2[p]) for p in pairs]
        span *= 2
        if span < chunk:
            a_pow = [_dot(a_pow[p], a_pow[p]) for p in pairs]
    y2 = [from_state[p][two:] + _dot(jnp.where(incl, gram[p][two:, :two], 0.0), u2[p])
          + _dot(jnp.where(incl, gram[p][two:, two:], 0.0), v2[p]) for p in pairs]
    for p in pairs:
        h_ref[p] = h_t[p] * decay_end[p] + _dot_tn(jnp.concatenate([u2[p], v2[p]], axis=0), bk[p])

    y = [y2[p][:chunk] + y2[p][chunk:] for p in pairs]
    mean = [_dot_sel(y[p], block_ones) * (1.0 / HEAD_DIM) for p in pairs]
    d = [y[p] - mean[p] for p in pairs]
    var = [_dot_sel(d[p] * d[p], block_ones) * (1.0 / HEAD_DIM) for p in pairs]
    rk = [_dot_sel(bonus[p][0], block_ones) for p in pairs]
    for p, (bi, sl) in zip(pairs, where):
        yn = d[p] * lax.rsqrt(var[p] + GN_EPS) * lnx_g[:, sl] + lnx_b[:, sl]
        o_ref[bi, :, sl] = ((yn + rk[p] * bonus[p][1]) * g_ref[bi, :, sl]).astype(o_ref.dtype)


def _wkv_scan(r, lw, k, v, a, b, g, vec, batch, seq):
    m = r.shape[0]
    assert 2 * WKV_CHUNK == LANES and seq % WKV_CHUNK == 0
    nc = seq // WKV_CHUNK
    rows = 2 if batch % 2 == 0 else 1
    blk = pl.BlockSpec((rows, WKV_CHUNK, MIX_WIDTH), lambda bi, c: (bi, c, 0))
    per_batch = lambda t: t.reshape(batch, seq, MIX_WIDTH)
    out = pl.pallas_call(
        _wkv_kernel,
        out_shape=jax.ShapeDtypeStruct((batch, seq, MIX_WIDTH), BF16),
        grid=(batch // rows, nc),
        in_specs=[blk] * 7 + [pl.BlockSpec(vec.shape, lambda bi, c: (0, 0))],
        out_specs=blk,
        scratch_shapes=[pltpu.VMEM((rows * MIX_WIDTH // LANES, LANES, LANES), F32)],
        compiler_params=_params("arbitrary", "arbitrary"),
        name="wkv_scan",
    )(*(per_batch(t) for t in (r, lw, k, v, a, b, g)), vec)
    return out.reshape(m, MIX_WIDTH)


def _mem_attn_kernel(q_ref, mk_ref, mv_ref, o_ref):
    q, mk, mv = q_ref[...], mk_ref[...], mv_ref[...]
    head_of_lane = _iota((1, MEM_WIDTH), 1) // HEAD_DIM
    heads = range(MEM_WIDTH // HEAD_DIM)
    s = [_dot_nt(q, jnp.where(head_of_lane == h, mk, 0.0)) * (HEAD_DIM ** -0.5) for h in heads]
    e = [jnp.exp(x - jnp.max(x, axis=-1, keepdims=True)) for x in s]
    p = [x / jnp.sum(x, axis=-1, keepdims=True) for x in e]
    acc = jnp.zeros(q.shape, F32)
    for h in heads:
        acc = acc + _dot(p[h], jnp.where(head_of_lane == h, mv, 0.0))
    o_ref[...] = acc.astype(o_ref.dtype)


def _mem_attention(proj, q_col_block, mkv, batch, seq, mem_tokens):
    m = proj.shape[0]
    tq = min(512, seq)
    nq = seq // tq
    return pl.pallas_call(
        _mem_attn_kernel,
        out_shape=jax.ShapeDtypeStruct((m, MEM_WIDTH), BF16),
        grid=(batch, nq),
        in_specs=[pl.BlockSpec((tq, MEM_WIDTH), lambda b, i: (b * nq + i, q_col_block)),
                  pl.BlockSpec((mem_tokens, MEM_WIDTH), lambda b, i: (b, 0)),
                  pl.BlockSpec((mem_tokens, MEM_WIDTH), lambda b, i: (b, 1))],
        out_specs=pl.BlockSpec((tq, MEM_WIDTH), lambda b, i: (b * nq + i, 0)),
        compiler_params=_params("arbitrary", "arbitrary"),
        name="mem_attention",
    )(proj, mkv, mkv)


def _stick_kernel(q_ref, k_ref, v_ref, o_ref, kt0, kt1, v0, v1, later_ref):
    i = pl.program_id(2)
    blk = q_ref.shape[0]
    seq = k_ref.shape[0]
    rr, cc = _iota((blk, blk), 0), _iota((blk, blk), 1)

    @pl.when(i == 0)
    def _():
        head0_lane = _iota((1, LANES), 1) < HEAD_DIM
        head0_row = _iota((LANES, 1), 0) < HEAD_DIM
        step = min(seq, 512)
        for c in range(seq // step):
            rows = slice(c * step, (c + 1) * step)
            kt = k_ref[rows, :].T
            kt0[:, rows] = jnp.where(head0_row, kt, 0.0).astype(BF16)
            kt1[:, rows] = jnp.where(head0_row, 0.0, kt).astype(BF16)
            vf = v_ref[rows, :]
            v0[rows, :] = jnp.where(head0_lane, vf, 0.0).astype(BF16)
            v1[rows, :] = jnp.where(head0_lane, 0.0, vf).astype(BF16)
        later_ref[...] = (rr > cc).astype(BF16)

    q = (q_ref[...] * (HEAD_DIM ** -0.5)).astype(BF16)
    causal = cc < rr

    def blocks(js, state, diagonal_first):
        acc, carries = state[0], list(state[1:])
        chains = [(j, h) for j in range(len(js)) for h in (0, 1)]
        masks = [causal if diagonal_first and j == 0 else None for j, _ in chains]
        keys = [pl.ds(pl.multiple_of(j * blk, blk), blk) for j in js]
        later = later_ref[...]
        z, sp, within = {}, {}, {}
        for n in range(len(chains) + 2):
            if n < len(chains):
                j, h = chains[n]
                z[n] = _dot(q, (kt0, kt1)[h][:, keys[j]]) * LOG2E
                minus_abs = lax.bitcast_convert_type(lax.bitcast_convert_type(z[n], jnp.int32) | SIGN_BIT, F32)
                s = jnp.maximum(z[n], 0.0) + jnp.log2(1.0 + jnp.exp2(minus_abs))
                sp[n] = s if masks[n] is None else jnp.where(masks[n], s, 0.0)
            if 0 <= n - 1 < len(chains):
                within[n - 1] = _dot(sp[n - 1], later)
            if 0 <= n - 2:
                j, h = chains[n - 2]
                w = jnp.exp2(z[n - 2] - sp[n - 2] - within[n - 2] - carries[h])
                if masks[n - 2] is not None:
                    w = jnp.where(masks[n - 2], w, 0.0)
                acc = acc + _dot(w, (v0, v1)[h][keys[j], :])
                carries[h] = carries[h] + jnp.sum(sp[n - 2], axis=-1, keepdims=True)
        return (acc, *carries)

    group = 4
    first = jnp.where((i + 1) % group == 0, group, (i + 1) % group)
    zero = jnp.zeros((blk, 1), F32)
    state = (jnp.zeros((blk, LANES), F32), zero, zero)
    for size in range(1, group + 1):
        state = lax.fori_loop(0, (first == size).astype(jnp.int32),
                              lambda n, s, size=size: blocks([i - c for c in range(size)], s, True), state)
    state = lax.fori_loop(0, (i + 1 - first) // group,
                          lambda n, s: blocks([i - first - group * n - c for c in range(group)], s, False), state)
    o_ref[...] = state[0].astype(o_ref.dtype)


def _stick_attention(proj, kv, batch, seq):
    m = proj.shape[0]
    blk = min(ATT_BLOCK, seq)
    nq = seq // blk
    n_pairs = MIX_WIDTH // LANES
    return pl.pallas_call(
        _stick_kernel,
        out_shape=jax.ShapeDtypeStruct((m, MIX_WIDTH), BF16),
        grid=(batch, n_pairs, nq),
        in_specs=[pl.BlockSpec((blk, LANES), lambda b, p, i: (b * nq + i, p)),
                  pl.BlockSpec((seq, LANES), lambda b, p, i: (b, p)),
                  pl.BlockSpec((seq, LANES), lambda b, p, i: (b, n_pairs + p))],
        out_specs=pl.BlockSpec((blk, LANES), lambda b, p, i: (b * nq + i, p)),
        scratch_shapes=[pltpu.VMEM((LANES, seq), BF16)] * 2 + [pltpu.VMEM((seq, LANES), BF16)] * 2
                       + [pltpu.VMEM((blk, blk), BF16)],
        compiler_params=_params("arbitrary", "arbitrary", "arbitrary"),
        name="stick_attention",
    )(proj, kv, kv)


def _mix_ln_kernel(tok_ref, mo_ref, x_ref, wa_ref, wb_ref, ln_ref, o_ref, ob_ref, *, alpha):
    mix = _dot(tok_ref[...], wa_ref[...]) + _dot(mo_ref[...], wb_ref[...])
    y = _layer_norm(alpha * x_ref[...] + mix, ln_ref[0:1, :], ln_ref[1:2, :])
    o_ref[...] = y
    ob_ref[...] = y.astype(BF16)


def _mix_ln(tok, mo, x, w_o, ln, alpha):
    m, d = x.shape
    tm = min(512, m)
    row = lambda i: (i, 0)
    return pl.pallas_call(
        functools.partial(_mix_ln_kernel, alpha=alpha),
        out_shape=(jax.ShapeDtypeStruct((m, d), F32), jax.ShapeDtypeStruct((m, d), BF16)),
        grid=(m // tm,),
        in_specs=[pl.BlockSpec((tm, MIX_WIDTH), row), pl.BlockSpec((tm, MEM_WIDTH), row),
                  pl.BlockSpec((tm, d), row),
                  pl.BlockSpec((MIX_WIDTH, d), lambda i: (0, 0)),
                  pl.BlockSpec((MEM_WIDTH, d), lambda i: (MIX_WIDTH // MEM_WIDTH, 0)),
                  pl.BlockSpec(ln.shape, lambda i: (0, 0))],
        out_specs=(pl.BlockSpec((tm, d), row),) * 2,
        compiler_params=_params("arbitrary"),
        name="mix_ln",
    )(tok, mo, x, w_o, w_o, ln)


HI16 = -65536


def _pack_bf16_pairs(x):
    half = x.shape[1] // 2
    xb = x.astype(BF16).astype(F32)
    lo = lax.bitcast_convert_type(xb[:, :half], jnp.int32)
    hi = lax.bitcast_convert_type(xb[:, half:], jnp.int32)
    return (hi & HI16) | lax.shift_right_logical(lo, 16)


def _unpack_bf16_pairs(u):
    lo = lax.bitcast_convert_type(lax.shift_left(u, 16), F32)
    hi = lax.bitcast_convert_type(u & HI16, F32)
    return jnp.concatenate([lo, hi], axis=-1).astype(BF16)


def _router_kernel(x_ref, w_ref, comb_ref, info_ref, xpk_ref, cnt_ref, carry_ref, *, n_experts):
    @pl.when(pl.program_id(0) == 0)
    def _():
        carry_ref[...] = jnp.zeros_like(carry_ref)

    x = x_ref[...]
    logits = jnp.dot(x, w_ref[...], preferred_element_type=F32, precision=lax.Precision.HIGHEST)
    lane = _iota(logits.shape, 1).astype(F32)
    neg = jnp.float32(-jnp.inf)
    logits = jnp.where(lane < n_experts, logits, neg)
    m1 = jnp.max(logits, axis=-1, keepdims=True)
    i1 = jnp.min(jnp.where(logits == m1, lane, float(LANES)), axis=-1, keepdims=True)
    rest = jnp.where(lane == i1, neg, logits)
    m2 = jnp.max(rest, axis=-1, keepdims=True)
    i2 = jnp.min(jnp.where(rest == m2, lane, float(LANES)), axis=-1, keepdims=True)
    e2 = jnp.exp(m2 - m1)
    w1 = 1.0 / (1.0 + e2)
    w2 = e2 / (1.0 + e2)
    first, second = lane == i1, lane == i2
    comb_ref[...] = jnp.where(first, w1, 0.0) + jnp.where(second, w2, 0.0)

    sel = jnp.where(first, 1.0, 0.0) + jnp.where(second, 1.0, 0.0)
    rows = sel.shape[0]
    earlier = (_iota((rows, rows), 1) < _iota((rows, rows), 0)).astype(BF16)
    rank = _dot(earlier, sel) + carry_ref[0:1, :]
    total = carry_ref[0:1, :] + jnp.sum(sel, axis=0, keepdims=True)
    carry_ref[0:1, :] = total
    cnt_ref[...] = jnp.broadcast_to(total, cnt_ref.shape)
    rank1 = jnp.sum(jnp.where(first, rank, 0.0), axis=-1, keepdims=True)
    rank2 = jnp.sum(jnp.where(second, rank, 0.0), axis=-1, keepdims=True)
    info_ref[...] = jnp.where(lane == 0, i1, jnp.where(lane == 1, i2, jnp.where(lane == 2, rank1,
                              jnp.where(lane == 3, rank2, 0.0))))
    xpk_ref[...] = _pack_bf16_pairs(x)


def _router(x, w_router):
    m, d = x.shape
    n_experts = w_router.shape[1]
    w_pad = jnp.pad(w_router, ((0, 0), (0, LANES - n_experts)))
    tm = min(512, m)
    row = lambda i: (i, 0)
    return pl.pallas_call(
        functools.partial(_router_kernel, n_experts=n_experts),
        out_shape=(jax.ShapeDtypeStruct((m, LANES), F32), jax.ShapeDtypeStruct((m, LANES), F32),
                   jax.ShapeDtypeStruct((m, d // 2), jnp.int32), jax.ShapeDtypeStruct((8, LANES), F32)),
        grid=(m // tm,),
        in_specs=[pl.BlockSpec((tm, d), row), pl.BlockSpec((d, LANES), lambda i: (0, 0))],
        out_specs=(pl.BlockSpec((tm, LANES), row), pl.BlockSpec((tm, LANES), row),
                   pl.BlockSpec((tm, d // 2), row), pl.BlockSpec((8, LANES), lambda i: (0, 0))),
        scratch_shapes=[pltpu.VMEM((8, LANES), F32)],
        compiler_params=_params("arbitrary"),
        name="router",
    )(x, w_pad)


def _moe_src_kernel(pos_ref, src_ref):
    def clear(i, c):
        src_ref[i] = 0
        return c

    lax.fori_loop(0, src_ref.shape[0], clear, 0, unroll=16)

    def place(t, c):
        src_ref[pos_ref[2 * t]] = t
        src_ref[pos_ref[2 * t + 1]] = t
        return c

    lax.fori_loop(0, pos_ref.shape[0] // 2, place, 0, unroll=8)


def _moe_src(pos, n_rows):
    smem = pl.BlockSpec(memory_space=pltpu.SMEM)
    return pl.pallas_call(
        _moe_src_kernel,
        out_shape=jax.ShapeDtypeStruct((n_rows,), jnp.int32),
        in_specs=[smem],
        out_specs=smem,
        name="moe_src",
    )(pos)


def _moe_dispatch_kernel(src_ref, xpk_ref, comb_ref, xs_ref, ws_ref):
    rows = xs_ref.shape[0]
    base = pl.program_id(0) * rows

    def copy(i, c):
        t = src_ref[base + i]
        xs_ref[pl.ds(i, 1), :] = xpk_ref[pl.ds(t, 1), :]
        ws_ref[pl.ds(i, 1), :] = comb_ref[pl.ds(t, 1), :]
        return c

    lax.fori_loop(0, rows, copy, 0, unroll=8)


def _moe_dispatch(src, xpk, comb, n_tiles):
    m, half = xpk.shape
    whole = lambda a: pl.BlockSpec(a.shape, lambda r, s: (0, 0), pipeline_mode=pl.Buffered(1))
    tile = lambda w: pl.BlockSpec((MOE_TILE, w), lambda r, s: (r, 0))
    return pl.pallas_call(
        _moe_dispatch_kernel,
        out_shape=(jax.ShapeDtypeStruct((n_tiles * MOE_TILE, half), jnp.int32),
                   jax.ShapeDtypeStruct((n_tiles * MOE_TILE, LANES), F32)),
        grid_spec=pltpu.PrefetchScalarGridSpec(
            num_scalar_prefetch=1, grid=(n_tiles,),
            in_specs=[whole(xpk), whole(comb)],
            out_specs=(tile(half), tile(LANES))),
        compiler_params=_params("arbitrary"),
        name="moe_dispatch",
    )(src, xpk, comb)


def _moe_up_kernel(te_ref, tv_ref, xs_ref, ws_ref, wg_ref, wu_ref, o_ref):
    r = pl.program_id(1)

    @pl.when(tv_ref[r] == 1)
    def _():
        x = _unpack_bf16_pairs(xs_ref[...])
        gate = _dot(x, wg_ref[0])
        h = gate * _sigmoid(gate) * _dot(x, wu_ref[0])
        ws = ws_ref[...]
        mine = _iota(ws.shape, 1) == te_ref[r]
        o_ref[...] = (h * jnp.sum(jnp.where(mine, ws, 0.0), axis=-1, keepdims=True)).astype(o_ref.dtype)

    @pl.when(tv_ref[r] == 0)
    def _():
        o_ref[...] = jnp.zeros_like(o_ref)


def _moe_up(tile_expert, tile_valid, xs, ws, w_gate_up, w_base):
    n_rows, half = xs.shape
    d = 2 * half
    d_ff = w_gate_up.shape[2] // 2
    n_split = 2
    tn = d_ff // n_split
    return pl.pallas_call(
        _moe_up_kernel,
        out_shape=jax.ShapeDtypeStruct((n_rows, d_ff), BF16),
        grid_spec=pltpu.PrefetchScalarGridSpec(
            num_scalar_prefetch=2, grid=(n_split, n_rows // MOE_TILE),
            in_specs=[pl.BlockSpec((MOE_TILE, half), lambda n, r, te, tv: (r, 0)),
                      pl.BlockSpec((MOE_TILE, LANES), lambda n, r, te, tv: (r, 0)),
                      pl.BlockSpec((1, d, tn), lambda n, r, te, tv: (w_base + te[r], 0, n)),
                      pl.BlockSpec((1, d, tn), lambda n, r, te, tv: (w_base + te[r], 0, n + n_split))],
            out_specs=pl.BlockSpec((MOE_TILE, tn), lambda n, r, te, tv: (r, n))),
        compiler_params=_params("arbitrary", "arbitrary"),
        name="moe_up",
    )(tile_expert, tile_valid, xs, ws, w_gate_up, w_gate_up)


def _moe_down_kernel(te_ref, tv_ref, h_ref, wd_ref, o_ref):
    r = pl.program_id(0)

    @pl.when(tv_ref[r] == 1)
    def _():
        o_ref[...] = _dot(h_ref[...], wd_ref[0]).astype(o_ref.dtype)

    @pl.when(tv_ref[r] == 0)
    def _():
        o_ref[...] = jnp.zeros_like(o_ref)


def _moe_down(tile_expert, tile_valid, h, w_down, w_base):
    n_rows, d_ff = h.shape
    d = w_down.shape[2]
    return pl.pallas_call(
        _moe_down_kernel,
        out_shape=jax.ShapeDtypeStruct((n_rows, d), BF16),
        grid_spec=pltpu.PrefetchScalarGridSpec(
            num_scalar_prefetch=2, grid=(n_rows // MOE_TILE,),
            in_specs=[pl.BlockSpec((MOE_TILE, d_ff), lambda r, te, tv: (r, 0)),
                      pl.BlockSpec((1, d_ff, d), lambda r, te, tv: (w_base + te[r], 0, 0))],
            out_specs=pl.BlockSpec((MOE_TILE, d), lambda r, te, tv: (r, 0))),
        compiler_params=_params("arbitrary"),
        name="moe_down",
    )(tile_expert, tile_valid, h, w_down)


def _moe_combine_kernel(win_ref, ok_ref, rs_ref, info_ref, x_ref, ln_ref, *refs, alpha, n_experts):
    n_win = 2 * n_experts
    windows, (o_ref, ob_ref) = refs[:n_win], refs[n_win:]
    i = pl.program_id(0)
    info = info_ref[...]
    i1, i2, rank1, rank2 = (info[:, c:c + 1] for c in range(4))
    pos1, pos2 = rank1, rank2
    for e in range(n_experts):
        start = rs_ref[e].astype(F32)
        pos1 = pos1 + jnp.where(i1 == e, start, 0.0)
        pos2 = pos2 + jnp.where(i2 == e, start, 0.0)
    rows = windows[0].shape[0]
    lane = _iota((1, rows), 1).astype(F32)
    acc = None
    for k in range(n_win):
        base = jnp.where(ok_ref[i * n_win + k] == 1, win_ref[i * n_win + k] * rows, -rows).astype(F32)
        pick = jnp.where(pos1 - base == lane, 1.0, jnp.where(pos2 - base == lane, 1.0, 0.0))
        part = _dot(pick, windows[k][...])
        acc = part if acc is None else acc + part
    y = _layer_norm(alpha * x_ref[...] + acc, ln_ref[0:1, :], ln_ref[1:2, :])
    o_ref[...] = y
    ob_ref[...] = y.astype(BF16)


def _moe_combine(win, ok, row_start, info, x, ln, out, alpha, n_experts):
    m, d = x.shape
    tt = COMBINE_TILE
    n_win = 2 * n_experts
    row = lambda i, *_: (i, 0)
    window = lambda k: pl.BlockSpec((tt, d), lambda i, w, o, s: (w[i * n_win + k], 0))
    return pl.pallas_call(
        functools.partial(_moe_combine_kernel, alpha=alpha, n_experts=n_experts),
        out_shape=(jax.ShapeDtypeStruct((m, d), F32), jax.ShapeDtypeStruct((m, d), BF16)),
        grid_spec=pltpu.PrefetchScalarGridSpec(
            num_scalar_prefetch=3, grid=(m // tt,),
            in_specs=[pl.BlockSpec((tt, LANES), row), pl.BlockSpec((tt, d), row),
                      pl.BlockSpec(ln.shape, lambda i, *_: (0, 0))] + [window(k) for k in range(n_win)],
            out_specs=(pl.BlockSpec((tt, d), row),) * 2),
        compiler_params=_params("arbitrary"),
        name="moe_combine",
    )(win, ok, row_start, info, x, ln, *([out] * n_win))


def _moe_swiglu(x, w_router, w_gate_up, w_down, w_base, ln, alpha):
    m, d = x.shape
    n_experts = w_router.shape[1]
    assert (TOP_K * m) % MOE_TILE == 0 and m % COMBINE_TILE == 0 and MOE_TILE % COMBINE_TILE == 0
    n_tiles = TOP_K * m // MOE_TILE + n_experts
    comb, info, xpk, counts = _router(x, w_router)

    i32 = jnp.int32
    experts = jnp.arange(n_experts, dtype=i32)
    counts = counts[0, :n_experts].astype(i32)
    tiles = (counts + MOE_TILE - 1) // MOE_TILE
    tile_end = jnp.cumsum(tiles)
    row_start = (tile_end - tiles) * MOE_TILE
    t = jnp.arange(n_tiles, dtype=i32)
    tile_expert = jnp.minimum(jnp.sum(t[:, None] >= tile_end[None, :], axis=1), n_experts - 1).astype(i32)
    tile_valid = (t < tile_end[-1]).astype(i32)
    i1, i2 = info[:, 0].astype(i32), info[:, 1].astype(i32)
    hit1, hit2 = i1[:, None] == experts[None, :], i2[:, None] == experts[None, :]
    pos1 = info[:, 2].astype(i32) + jnp.sum(jnp.where(hit1, row_start[None, :], 0), axis=1)
    pos2 = info[:, 3].astype(i32) + jnp.sum(jnp.where(hit2, row_start[None, :], 0), axis=1)
    per_tile = jnp.sum((hit1 | hit2).astype(i32).reshape(m // COMBINE_TILE, COMBINE_TILE, n_experts), axis=1)
    first = row_start[None, :] + jnp.cumsum(per_tile, axis=0) - per_tile
    win0 = first // COMBINE_TILE
    win1 = (first + jnp.maximum(per_tile, 1) - 1) // COMBINE_TILE
    ok0 = per_tile > 0
    ok1 = ok0 & (win1 != win0)
    win = jnp.stack([jnp.where(ok0, win0, 0), jnp.where(ok1, win1, 0)], axis=-1).reshape(-1).astype(i32)
    ok = jnp.stack([ok0, ok1], axis=-1).reshape(-1).astype(i32)

    src = _moe_src(jnp.stack([pos1, pos2], axis=-1).reshape(-1), n_tiles * MOE_TILE)
    xs, ws = _moe_dispatch(src, xpk, comb, n_tiles)
    h = _moe_up(tile_expert, tile_valid, xs, ws, w_gate_up, w_base)
    out = _moe_down(tile_expert, tile_valid, h, w_down, w_base)
    return _moe_combine(win, ok, row_start, info, x, ln, out, alpha, n_experts)


def _ffn_up_kernel(x_ref, wg_ref, wu_ref, o_ref):
    x = x_ref[...]
    gate = _dot(x, wg_ref[0])
    o_ref[0] = (gate * _sigmoid(gate) * _dot(x, wu_ref[0])).astype(o_ref.dtype)


def _ffn_up(xb, w_gate_up):
    m, d = xb.shape
    n_e = w_gate_up.shape[0]
    d_ff = w_gate_up.shape[2] // 2
    n_split = 2
    tn = d_ff // n_split
    tm = min(512, m)
    return pl.pallas_call(
        _ffn_up_kernel,
        out_shape=jax.ShapeDtypeStruct((n_e, m, d_ff), BF16),
        grid=(n_e, n_split, m // tm),
        in_specs=[pl.BlockSpec((tm, d), lambda e, n, i: (i, 0)),
                  pl.BlockSpec((1, d, tn), lambda e, n, i: (e, 0, n)),
                  pl.BlockSpec((1, d, tn), lambda e, n, i: (e, 0, n + n_split))],
        out_specs=pl.BlockSpec((1, tm, tn), lambda e, n, i: (e, i, n)),
        compiler_params=_params("arbitrary", "arbitrary", "arbitrary"),
        name="ffn_up",
    )(xb, w_gate_up, w_gate_up)


def _ffn_down_kernel(h_ref, wd_ref, x_ref, ln_ref, o_ref, ob_ref, acc_ref, *, alpha):
    e = pl.program_id(1)

    @pl.when(e == 0)
    def _():
        acc_ref[...] = jnp.zeros_like(acc_ref)

    acc_ref[...] += _dot(h_ref[0], wd_ref[0])

    @pl.when(e == pl.num_programs(1) - 1)
    def _():
        y = _layer_norm(alpha * x_ref[...] + acc_ref[...], ln_ref[0:1, :], ln_ref[1:2, :])
        o_ref[...] = y
        ob_ref[...] = y.astype(BF16)


def _ffn_down(h, w_down, x, ln, alpha):
    n_e, m, d_ff = h.shape
    d = x.shape[1]
    tm = min(512, m)
    row = lambda i, e: (i, 0)
    return pl.pallas_call(
        functools.partial(_ffn_down_kernel, alpha=alpha),
        out_shape=(jax.ShapeDtypeStruct((m, d), F32), jax.ShapeDtypeStruct((m, d), BF16)),
        grid=(m // tm, n_e),
        in_specs=[pl.BlockSpec((1, tm, d_ff), lambda i, e: (e, i, 0)),
                  pl.BlockSpec((1, d_ff, d), lambda i, e: (e, 0, 0)),
                  pl.BlockSpec((tm, d), row),
                  pl.BlockSpec(ln.shape, lambda i, e: (0, 0))],
        out_specs=(pl.BlockSpec((tm, d), row),) * 2,
        scratch_shapes=[pltpu.VMEM((tm, d), F32)],
        compiler_params=_params("arbitrary", "arbitrary"),
        name="ffn_down",
    )(h, w_down, x, ln)


def _pad_rows(w, rows_before, rows_total):
    return jnp.pad(w, ((rows_before, rows_total - rows_before - w.shape[0]), (0, 0)))


def kernel(x, mem, a_w_in_first, a_w_in_rest, a_mu_first, a_mu_rest, a_vec, a_w_up, a_a_up, a_g_up, a_v0, a_v_up, a_r_k, b_w_in, w_kv_shared, mem_kv, w_o, ln, ffn_gate_up, ffn_down, router, exp_gate_up, exp_down):
    batch, seq, d_model = x.shape
    mem_tokens = mem.shape[1]
    depth = w_o.shape[0]
    n_a = a_vec.shape[0]
    alpha = (2 * depth) ** 0.25
    m = batch * seq
    rkv_w = 3 * MIX_WIDTH
    lora_w = DECAY_LORA + ICLR_LORA + GATE_LORA
    assert DECAY_LORA + ICLR_LORA == LANES and lora_w == 2 * LANES and rkv_w % lora_w == 0

    xf = x.reshape(m, d_model)
    xb = xf.astype(BF16)
    memb = mem.reshape(batch * mem_tokens, d_model).astype(BF16)
    hh = jnp.arange(MIX_WIDTH) // HEAD_DIM
    block_ones = (hh[:, None] == hh[None, :]).astype(BF16)

    v_first = None
    kv = None
    for l in range(depth):
        mkv = _matmul(memb, mem_kv[l].astype(BF16), tn=2 * MEM_WIDTH)
        if l < n_a:
            w_in = a_w_in_first if l == 0 else a_w_in_rest[l - 1]
            mu = a_mu_first if l == 0 else a_mu_rest[l - 1]
            has_vres = l > 0
            n_cols = w_in.shape[1] - MEM_WIDTH
            gate_lo = n_cols - GATE_LORA
            order = jnp.concatenate([jnp.arange(rkv_w + DECAY_LORA + ICLR_LORA), jnp.arange(gate_lo, n_cols),
                                     jnp.arange(n_cols, n_cols + MEM_WIDTH)])
            proj = _matmul(xb, w_in[:, order].astype(BF16), tn=(rkv_w + lora_w + MEM_WIDTH) // 2)
            mu_rkv = mu[:rkv_w].reshape(1, rkv_w)
            mu_lora = jnp.concatenate([mu[rkv_w:rkv_w + LANES], mu[gate_lo:n_cols]]).reshape(1, lora_w)
            ww = _pad_rows(a_w_up[l], 0, LANES).astype(BF16)
            wa = _pad_rows(a_a_up[l], DECAY_LORA, LANES).astype(BF16)
            wg = a_g_up[l].astype(BF16)
            vec = jnp.concatenate([a_vec[l, :4], a_v0[l - 1][None] if has_vres else jnp.zeros((1, MIX_WIDTH), F32),
                                   jnp.zeros((3, MIX_WIDTH), F32)])
            if has_vres:
                vd_lo = rkv_w + LANES
                w_vd = jnp.pad(w_in[:, vd_lo:vd_lo + VALUE_LORA], ((0, 0), (0, LANES - VALUE_LORA)))
                vd = _matmul(xb, w_vd.astype(BF16), tn=LANES)
                mu_vd = jnp.pad(mu[vd_lo:vd_lo + VALUE_LORA], (0, LANES - VALUE_LORA)).reshape(1, LANES)
                wv = _pad_rows(a_v_up[l - 1], 0, LANES).astype(BF16)
                ops = _rwkv_prep(proj, vd, v_first, mu_rkv, mu_lora, mu_vd, vec, ww, wa, wg, wv, block_ones,
                                 batch, seq)
            else:
                ops = _rwkv_prep(proj, None, None, mu_rkv, mu_lora, None, vec, ww, wa, wg, None, block_ones,
                                 batch, seq)
                v_first = ops[3]
            scan_vec = jnp.concatenate([a_vec[l, 4:6], a_r_k[l].reshape(1, MIX_WIDTH),
                                        jnp.zeros((5, MIX_WIDTH), F32)])
            tok = _wkv_scan(*ops, scan_vec, batch, seq)
            q_col_block = (rkv_w + lora_w) // MEM_WIDTH
        else:
            proj = _matmul(xb, b_w_in[l - n_a].astype(BF16), tn=(MIX_WIDTH + MEM_WIDTH) // 2)
            tok = _stick_attention(proj, kv, batch, seq)
            q_col_block = MIX_WIDTH // MEM_WIDTH
        mo = _mem_attention(proj, q_col_block, mkv, batch, seq, mem_tokens)
        xf, xb = _mix_ln(tok, mo, xf, w_o[l].astype(BF16), ln[l, 0:2], alpha)
        if l % 2 == 0:
            h = _ffn_up(xb, ffn_gate_up[l // 2][None].astype(BF16))
            xf, xb = _ffn_down(h, ffn_down[l // 2][None].astype(BF16), xf, ln[l, 2:4], alpha)
        else:
            n_e = router.shape[2]
            xf, xb = _moe_swiglu(xf, router[l // 2], exp_gate_up.reshape((-1,) + exp_gate_up.shape[2:]),
                                 exp_down.reshape((-1,) + exp_down.shape[2:]), (l // 2) * n_e, ln[l, 2:4], alpha)
        if l == n_a - 1:
            kv = _matmul(xb, w_kv_shared.astype(BF16), tn=MIX_WIDTH)
    return xf.reshape(batch, seq, d_model)
```

```python
import functools

import jax
import jax.numpy as jnp
from jax import lax
from jax.experimental import pallas as pl
from jax.experimental.pallas import tpu as pltpu

F32 = jnp.float32
BF16 = jnp.bfloat16

HEAD_DIM = 64
LANES = 128
MIX_WIDTH = 768
MEM_WIDTH = 256
DECAY_LORA = 64
ICLR_LORA = 64
VALUE_LORA = 32
GATE_LORA = 128
TOP_K = 2
LN_EPS = 1e-5
GN_EPS = 64e-5
WKV_CHUNK = 64
ATT_BLOCK = 256
LOG2E = 1.4426950408889634
SIGN_BIT = -2147483648
MOE_TILE = 512
COMBINE_TILE = 256
VMEM_LIMIT = 56 * 1024 * 1024


def _params(*sem):
    return pltpu.CompilerParams(dimension_semantics=sem, vmem_limit_bytes=VMEM_LIMIT)


def _dot(a, b):
    return jnp.dot(a.astype(BF16), b.astype(BF16), preferred_element_type=F32)


def _dot_nt(a, b):
    return lax.dot_general(a.astype(BF16), b.astype(BF16), (((1,), (1,)), ((), ())),
                           preferred_element_type=F32)


def _dot_tn(a, b):
    return lax.dot_general(a.astype(BF16), b.astype(BF16), (((0,), (0,)), ((), ())),
                           preferred_element_type=F32)


def _split3(x):
    x1 = x.astype(BF16)
    r1 = x - x1.astype(F32)
    x2 = r1.astype(BF16)
    x3 = (r1 - x2.astype(F32)).astype(BF16)
    return x1, x2, x3


def _dot_sel(x, sel):
    x1 = x.astype(BF16)
    x2 = (x - x1.astype(F32)).astype(BF16)
    return _dot(x1, sel) + _dot(x2, sel)


def _sel_dot(sel, x):
    x1, x2, x3 = _split3(x)
    return _dot(sel, x1) + _dot(sel, x2) + _dot(sel, x3)


def _softplus(z):
    return jnp.maximum(z, 0.0) + jnp.log1p(jnp.exp(-jnp.abs(z)))


def _sigmoid(z):
    return 1.0 / (1.0 + jnp.exp(-z))


def _layer_norm(y, g, b):
    mu = jnp.mean(y, axis=-1, keepdims=True)
    d = y - mu
    var = jnp.mean(d * d, axis=-1, keepdims=True)
    return d * lax.rsqrt(var + LN_EPS) * g + b


def _iota(shape, dim):
    return lax.broadcasted_iota(jnp.int32, shape, dim)


def _mm_kernel(x_ref, w_ref, o_ref):
    o_ref[...] = _dot(x_ref[...], w_ref[...]).astype(o_ref.dtype)


def _matmul(x, w, tn, out_dtype=F32, tm=512):
    m, k = x.shape
    n = w.shape[1]
    tm = min(tm, m)
    return pl.pallas_call(
        _mm_kernel,
        out_shape=jax.ShapeDtypeStruct((m, n), out_dtype),
        grid=(n // tn, m // tm),
        in_specs=[pl.BlockSpec((tm, k), lambda j, i: (i, 0)),
                  pl.BlockSpec((k, tn), lambda j, i: (0, j))],
        out_specs=pl.BlockSpec((tm, tn), lambda j, i: (i, j)),
        compiler_params=_params("arbitrary", "arbitrary"),
        name="matmul",
    )(x, w)


def _prep_kernel(*refs, has_vres):
    if has_vres:
        (rkv_ref, lora_ref, vd_ref, vf_ref, mu_rkv_ref, mu_lora_ref, mu_vd_ref, vec_ref,
         ww_ref, wa_ref, wg_ref, wv_ref, bo_ref,
         r_o, lw_o, k_o, v_o, a_o, b_o, g_o, c_rkv, c_lora, c_vd) = refs
    else:
        (rkv_ref, lora_ref, mu_rkv_ref, mu_lora_ref, vec_ref,
         ww_ref, wa_ref, wg_ref, bo_ref,
         r_o, lw_o, k_o, v_o, a_o, b_o, g_o, c_rkv, c_lora) = refs

    t = pl.program_id(1)
    carries = (c_rkv, c_lora) + ((c_vd,) if has_vres else ())

    @pl.when(t == 0)
    def _():
        for c in carries:
            c[...] = jnp.zeros_like(c)

    def shifted(x_ref, mu_ref, carry_ref):
        cur = x_ref[...]
        rows = cur.shape[0]
        rolled = pltpu.roll(cur, 1, 0)
        prev = jnp.where(_iota(cur.shape, 0) == 0, carry_ref[0:1, :], rolled)
        carry_ref[0:1, :] = cur[rows - 1:rows, :]
        return cur + mu_ref[...] * (prev - cur)

    xs = shifted(rkv_ref, mu_rkv_ref, c_rkv)
    xl = shifted(lora_ref, mu_lora_ref, c_lora)
    r = xs[:, :MIX_WIDTH]
    k = xs[:, MIX_WIDTH:2 * MIX_WIDTH]
    v = xs[:, 2 * MIX_WIDTH:]
    wa_in = xl[:, :LANES]
    gd = xl[:, LANES:]
    w0, a0, k_k, k_a = (vec_ref[i:i + 1, :] for i in range(4))

    log_w = -_softplus(-(w0 + _dot(jnp.tanh(wa_in), ww_ref[...]))) - 0.5
    lw_o[...] = -jnp.exp(log_w)
    a_lr = _sigmoid(a0 + _dot(wa_in, wa_ref[...]))
    if has_vres:
        xv = shifted(vd_ref, mu_vd_ref, c_vd)
        v0 = vec_ref[4:5, :]
        v = v + (vf_ref[...] - v) * _sigmoid(v0 + _dot(xv, wv_ref[...]))
    g_o[...] = _dot(_sigmoid(gd), wg_ref[...])
    kk = k * k_k
    norm = jnp.sqrt(_dot_sel(kk * kk, bo_ref[...]))
    kk = kk / jnp.maximum(norm, 1e-12)
    r_o[...] = r
    k_o[...] = k * (1.0 + (a_lr - 1.0) * k_a)
    v_o[...] = v
    a_o[...] = -kk
    b_o[...] = kk * a_lr


def _rwkv_prep(proj, vd, v_first, mu_rkv, mu_lora, mu_vd, vec, ww, wa, wg, wv, bo, batch, seq):
    has_vres = vd is not None
    m = proj.shape[0]
    tt = min(256, seq)
    nt = seq // tt
    rkv_w = 3 * MIX_WIDTH
    row = lambda b, t: (b * nt + t, 0)
    const = lambda b, t: (0, 0)
    full = lambda a: pl.BlockSpec(a.shape, const)
    in_specs = [pl.BlockSpec((tt, rkv_w), row),
                pl.BlockSpec((tt, 2 * LANES), lambda b, t: (b * nt + t, rkv_w // (2 * LANES)))]
    args = [proj, proj]
    if has_vres:
        in_specs += [pl.BlockSpec((tt, LANES), row), pl.BlockSpec((tt, MIX_WIDTH), row)]
        args += [vd, v_first]
    in_specs += [full(mu_rkv), full(mu_lora)]
    args += [mu_rkv, mu_lora]
    if has_vres:
        in_specs.append(full(mu_vd))
        args.append(mu_vd)
    in_specs += [full(vec), full(ww), full(wa), full(wg)]
    args += [vec, ww, wa, wg]
    if has_vres:
        in_specs.append(full(wv))
        args.append(wv)
    in_specs.append(full(bo))
    args.append(bo)
    scratch = [pltpu.VMEM((8, rkv_w), F32), pltpu.VMEM((8, 2 * LANES), F32)]
    if has_vres:
        scratch.append(pltpu.VMEM((8, LANES), F32))
    out = jax.ShapeDtypeStruct((m, MIX_WIDTH), F32)
    return pl.pallas_call(
        functools.partial(_prep_kernel, has_vres=has_vres),
        out_shape=(out,) * 7,
        grid=(batch, nt),
        in_specs=in_specs,
        out_specs=(pl.BlockSpec((tt, MIX_WIDTH), row),) * 7,
        scratch_shapes=scratch,
        compiler_params=_params("arbitrary", "arbitrary"),
        name="rwkv_prep",
    )(*args)


def _wkv_kernel(r_ref, lw_ref, k_ref, v_ref, a_ref, b_ref, g_ref, vec_ref, o_ref, h_ref):
    c = pl.program_id(1)
    chunk = r_ref.shape[1]
    two = 2 * chunk

    @pl.when(c == 0)
    def _():
        h_ref[...] = jnp.zeros_like(h_ref)

    tri = (_iota((chunk, chunk), 1) <= _iota((chunk, chunk), 0)).astype(BF16)
    log_p_all = [_sel_dot(tri, lw_ref[bi]) for bi in range(r_ref.shape[0])]
    head0 = _iota((1, LANES), 1) < HEAD_DIM
    rr, cc = _iota((two, two), 0), _iota((two, two), 1)
    strict, incl = cc < rr, cc <= rr
    block_ones = ((rr // HEAD_DIM) == (cc // HEAD_DIM)).astype(BF16)
    lnx_g, lnx_b, r_k = (vec_ref[i:i + 1, :] for i in range(3))

    def stack(x):
        return jnp.concatenate([jnp.where(head0, x, 0.0), jnp.where(head0, 0.0, x)], axis=0)

    n_pairs = r_ref.shape[2] // LANES
    where = [(bi, slice(p * LANES, (p + 1) * LANES)) for bi in range(r_ref.shape[0]) for p in range(n_pairs)]
    pairs = range(len(where))
    lhs, rhs, bk, v2, decay_end, bonus = [], [], [], [], [], []
    for bi, sl in where:
        lw, log_p = lw_ref[bi, :, sl], log_p_all[bi][:, sl]
        log_p_end = log_p[chunk - 1:chunk, :]
        inv_p = jnp.exp(-log_p)
        to_end = jnp.exp(log_p_end - log_p)
        r, k, v, a, b = r_ref[bi, :, sl], k_ref[bi, :, sl], v_ref[bi, :, sl], a_ref[bi, :, sl], b_ref[bi, :, sl]
        lhs.append(jnp.concatenate([stack(a * jnp.exp(log_p - lw)), stack(r * jnp.exp(log_p))], axis=0))
        rhs.append(jnp.concatenate([stack(b * inv_p), stack(k * inv_p)], axis=0))
        bk.append(jnp.concatenate([stack(b * to_end), stack(k * to_end)], axis=0))
        v2.append(stack(v))
        decay_end.append(jnp.exp(log_p_end))
        bonus.append((r * k * r_k[:, sl], v))
    gram = [_dot_nt(lhs[p], rhs[p]) for p in pairs]
    h_t = [h_ref[p] for p in pairs]
    from_state = [_dot_nt(lhs[p], h_t[p]) for p in pairs]
    a_pow = [jnp.where(strict, gram[p][:two, :two], 0.0) for p in pairs]
    u2 = [from_state[p][:two] + _dot(jnp.where(strict, gram[p][:two, two:], 0.0), v2[p]) for p in pairs]
    span = 1
    while span < chunk:
        u2 = [u2[p] + _dot(a_pow[p], u2[p]) for p in pairs]
        span *= 2
        if span < chunk:
            a_pow = [_dot(a_pow[p], a_pow[p]) for p in pairs]
    y2 = [from_state[p][two:] + _dot(jnp.where(incl, gram[p][two:, :two], 0.0), u2[p])
          + _dot(jnp.where(incl, gram[p][two:, two:], 0.0), v2[p]) for p in pairs]
    for p in pairs:
        h_ref[p] = h_t[p] * decay_end[p] + _dot_tn(jnp.concatenate([u2[p], v2[p]], axis=0), bk[p])

    y = [y2[p][:chunk] + y2[p][chunk:] for p in pairs]
    mean = [_dot_sel(y[p], block_ones) * (1.0 / HEAD_DIM) for p in pairs]
    d = [y[p] - mean[p] for p in pairs]
    var = [_dot_sel(d[p] * d[p], block_ones) * (1.0 / HEAD_DIM) for p in pairs]
    rk = [_dot_sel(bonus[p][0], block_ones) for p in pairs]
    for p, (bi, sl) in zip(pairs, where):
        yn = d[p] * lax.rsqrt(var[p] + GN_EPS) * lnx_g[:, sl] + lnx_b[:, sl]
        o_ref[bi, :, sl] = ((yn + rk[p] * bonus[p][1]) * g_ref[bi, :, sl]).astype(o_ref.dtype)


def _wkv_scan(r, lw, k, v, a, b, g, vec, batch, seq):
    m = r.shape[0]
    assert 2 * WKV_CHUNK == LANES and seq % WKV_CHUNK == 0
    nc = seq // WKV_CHUNK
    rows = next(n for n in (4, 2, 1) if batch % n == 0)
    blk = pl.BlockSpec((rows, WKV_CHUNK, MIX_WIDTH), lambda bi, c: (bi, c, 0))
    per_batch = lambda t: t.reshape(batch, seq, MIX_WIDTH)
    out = pl.pallas_call(
        _wkv_kernel,
        out_shape=jax.ShapeDtypeStruct((batch, seq, MIX_WIDTH), BF16),
        grid=(batch // rows, nc),
        in_specs=[blk] * 7 + [pl.BlockSpec(vec.shape, lambda bi, c: (0, 0))],
        out_specs=blk,
        scratch_shapes=[pltpu.VMEM((rows * MIX_WIDTH // LANES, LANES, LANES), F32)],
        compiler_params=_params("arbitrary", "arbitrary"),
        name="wkv_scan",
    )(*(per_batch(t) for t in (r, lw, k, v, a, b, g)), vec)
    return out.reshape(m, MIX_WIDTH)


def _mem_attn_kernel(q_ref, mk_ref, mv_ref, o_ref):
    q, mk, mv = q_ref[...], mk_ref[...], mv_ref[...]
    head_of_lane = _iota((1, MEM_WIDTH), 1) // HEAD_DIM
    heads = range(MEM_WIDTH // HEAD_DIM)
    s = [_dot_nt(q, jnp.where(head_of_lane == h, mk, 0.0)) * (HEAD_DIM ** -0.5) for h in heads]
    e = [jnp.exp(x - jnp.max(x, axis=-1, keepdims=True)) for x in s]
    p = [x / jnp.sum(x, axis=-1, keepdims=True) for x in e]
    acc = jnp.zeros(q.shape, F32)
    for h in heads:
        acc = acc + _dot(p[h], jnp.where(head_of_lane == h, mv, 0.0))
    o_ref[...] = acc.astype(o_ref.dtype)


def _mem_attention(proj, q_col_block, mkv, batch, seq, mem_tokens):
    m = proj.shape[0]
    tq = min(512, seq)
    nq = seq // tq
    return pl.pallas_call(
        _mem_attn_kernel,
        out_shape=jax.ShapeDtypeStruct((m, MEM_WIDTH), BF16),
        grid=(batch, nq),
        in_specs=[pl.BlockSpec((tq, MEM_WIDTH), lambda b, i: (b * nq + i, q_col_block)),
                  pl.BlockSpec((mem_tokens, MEM_WIDTH), lambda b, i: (b, 0)),
                  pl.BlockSpec((mem_tokens, MEM_WIDTH), lambda b, i: (b, 1))],
        out_specs=pl.BlockSpec((tq, MEM_WIDTH), lambda b, i: (b * nq + i, 0)),
        compiler_params=_params("arbitrary", "arbitrary"),
        name="mem_attention",
    )(proj, mkv, mkv)


def _stick_kernel(q_ref, k_ref, v_ref, o_ref, kt0, kt1, v0, v1, later_ref):
    i = pl.program_id(2)
    blk = q_ref.shape[0]
    seq = k_ref.shape[0]
    rr, cc = _iota((blk, blk), 0), _iota((blk, blk), 1)

    @pl.when(i == 0)
    def _():
        head0_lane = _iota((1, LANES), 1) < HEAD_DIM
        head0_row = _iota((LANES, 1), 0) < HEAD_DIM
        step = min(seq, 512)
        for c in range(seq // step):
            rows = slice(c * step, (c + 1) * step)
            kt = k_ref[rows, :].T
            kt0[:, rows] = jnp.where(head0_row, kt, 0.0).astype(BF16)
            kt1[:, rows] = jnp.where(head0_row, 0.0, kt).astype(BF16)
            vf = v_ref[rows, :]
            v0[rows, :] = jnp.where(head0_lane, vf, 0.0).astype(BF16)
            v1[rows, :] = jnp.where(head0_lane, 0.0, vf).astype(BF16)
        later_ref[...] = (rr > cc).astype(BF16)

    q = (q_ref[...] * (HEAD_DIM ** -0.5)).astype(BF16)
    causal = cc < rr

    def blocks(js, state, diagonal_first):
        acc, carries = state[0], list(state[1:])
        chains = [(j, h) for j in range(len(js)) for h in (0, 1)]
        masks = [causal if diagonal_first and j == 0 else None for j, _ in chains]
        keys = [pl.ds(pl.multiple_of(j * blk, blk), blk) for j in js]
        later = later_ref[...]
        z, sp, within = {}, {}, {}
        for n in range(len(chains) + 2):
            if n < len(chains):
                j, h = chains[n]
                z[n] = _dot(q, (kt0, kt1)[h][:, keys[j]]) * LOG2E
                minus_abs = lax.bitcast_convert_type(lax.bitcast_convert_type(z[n], jnp.int32) | SIGN_BIT, F32)
                s = jnp.maximum(z[n], 0.0) + jnp.log2(1.0 + jnp.exp2(minus_abs))
                sp[n] = s if masks[n] is None else jnp.where(masks[n], s, 0.0)
            if 0 <= n - 1 < len(chains):
                within[n - 1] = _dot(sp[n - 1], later)
            if 0 <= n - 2:
                j, h = chains[n - 2]
                w = jnp.exp2(z[n - 2] - sp[n - 2] - within[n - 2] - carries[h])
                if masks[n - 2] is not None:
                    w = jnp.where(masks[n - 2], w, 0.0)
                acc = acc + _dot(w, (v0, v1)[h][keys[j], :])
                carries[h] = carries[h] + jnp.sum(sp[n - 2], axis=-1, keepdims=True)
        return (acc, *carries)

    group = 6
    first = jnp.where((i + 1) % group == 0, group, (i + 1) % group)
    zero = jnp.zeros((blk, 1), F32)
    state = (jnp.zeros((blk, LANES), F32), zero, zero)
    for size in range(1, group + 1):
        state = lax.fori_loop(0, (first == size).astype(jnp.int32),
                              lambda n, s, size=size: blocks([i - c for c in range(size)], s, True), state)
    state = lax.fori_loop(0, (i + 1 - first) // group,
                          lambda n, s: blocks([i - first - group * n - c for c in range(group)], s, False), state)
    o_ref[...] = state[0].astype(o_ref.dtype)


def _stick_attention(proj, kv, batch, seq):
    m = proj.shape[0]
    blk = min(ATT_BLOCK, seq)
    nq = seq // blk
    n_pairs = MIX_WIDTH // LANES
    return pl.pallas_call(
        _stick_kernel,
        out_shape=jax.ShapeDtypeStruct((m, MIX_WIDTH), BF16),
        grid=(batch, n_pairs, nq),
        in_specs=[pl.BlockSpec((blk, LANES), lambda b, p, i: (b * nq + i, p)),
                  pl.BlockSpec((seq, LANES), lambda b, p, i: (b, p)),
                  pl.BlockSpec((seq, LANES), lambda b, p, i: (b, n_pairs + p))],
        out_specs=pl.BlockSpec((blk, LANES), lambda b, p, i: (b * nq + i, p)),
        scratch_shapes=[pltpu.VMEM((LANES, seq), BF16)] * 2 + [pltpu.VMEM((seq, LANES), BF16)] * 2
                       + [pltpu.VMEM((blk, blk), BF16)],
        compiler_params=_params("arbitrary", "arbitrary", "arbitrary"),
        name="stick_attention",
    )(proj, kv, kv)


def _mix_ln_kernel(tok_ref, mo_ref, x_ref, wa_ref, wb_ref, ln_ref, o_ref, ob_ref, *, alpha):
    mix = _dot(tok_ref[...], wa_ref[...]) + _dot(mo_ref[...], wb_ref[...])
    y = _layer_norm(alpha * x_ref[...] + mix, ln_ref[0:1, :], ln_ref[1:2, :])
    o_ref[...] = y
    ob_ref[...] = y.astype(BF16)


def _mix_ln(tok, mo, x, w_o, ln, alpha):
    m, d = x.shape
    tm = min(512, m)
    row = lambda i: (i, 0)
    return pl.pallas_call(
        functools.partial(_mix_ln_kernel, alpha=alpha),
        out_shape=(jax.ShapeDtypeStruct((m, d), F32), jax.ShapeDtypeStruct((m, d), BF16)),
        grid=(m // tm,),
        in_specs=[pl.BlockSpec((tm, MIX_WIDTH), row), pl.BlockSpec((tm, MEM_WIDTH), row),
                  pl.BlockSpec((tm, d), row),
                  pl.BlockSpec((MIX_WIDTH, d), lambda i: (0, 0)),
                  pl.BlockSpec((MEM_WIDTH, d), lambda i: (MIX_WIDTH // MEM_WIDTH, 0)),
                  pl.BlockSpec(ln.shape, lambda i: (0, 0))],
        out_specs=(pl.BlockSpec((tm, d), row),) * 2,
        compiler_params=_params("arbitrary"),
        name="mix_ln",
    )(tok, mo, x, w_o, w_o, ln)


HI16 = -65536


def _pack_bf16_pairs(x):
    half = x.shape[1] // 2
    xb = x.astype(BF16).astype(F32)
    lo = lax.bitcast_convert_type(xb[:, :half], jnp.int32)
    hi = lax.bitcast_convert_type(xb[:, half:], jnp.int32)
    return (hi & HI16) | lax.shift_right_logical(lo, 16)


def _unpack_bf16_pairs(u):
    lo = lax.bitcast_convert_type(lax.shift_left(u, 16), F32)
    hi = lax.bitcast_convert_type(u & HI16, F32)
    return jnp.concatenate([lo, hi], axis=-1).astype(BF16)


def _router_kernel(x_ref, w_ref, comb_ref, info_ref, xpk_ref, cnt_ref, carry_ref, *, n_experts):
    @pl.when(pl.program_id(0) == 0)
    def _():
        carry_ref[...] = jnp.zeros_like(carry_ref)

    x = x_ref[...]
    logits = jnp.dot(x, w_ref[...], preferred_element_type=F32, precision=lax.Precision.HIGHEST)
    lane = _iota(logits.shape, 1).astype(F32)
    neg = jnp.float32(-jnp.inf)
    logits = jnp.where(lane < n_experts, logits, neg)
    m1 = jnp.max(logits, axis=-1, keepdims=True)
    i1 = jnp.min(jnp.where(logits == m1, lane, float(LANES)), axis=-1, keepdims=True)
    rest = jnp.where(lane == i1, neg, logits)
    m2 = jnp.max(rest, axis=-1, keepdims=True)
    i2 = jnp.min(jnp.where(rest == m2, lane, float(LANES)), axis=-1, keepdims=True)
    e2 = jnp.exp(m2 - m1)
    w1 = 1.0 / (1.0 + e2)
    w2 = e2 / (1.0 + e2)
    first, second = lane == i1, lane == i2
    comb_ref[...] = jnp.where(first, w1, 0.0) + jnp.where(second, w2, 0.0)

    sel = jnp.where(first, 1.0, 0.0) + jnp.where(second, 1.0, 0.0)
    rows = sel.shape[0]
    earlier = (_iota((rows, rows), 1) < _iota((rows, rows), 0)).astype(BF16)
    rank = _dot(earlier, sel) + carry_ref[0:1, :]
    total = carry_ref[0:1, :] + jnp.sum(sel, axis=0, keepdims=True)
    carry_ref[0:1, :] = total
    cnt_ref[...] = jnp.broadcast_to(total, cnt_ref.shape)
    rank1 = jnp.sum(jnp.where(first, rank, 0.0), axis=-1, keepdims=True)
    rank2 = jnp.sum(jnp.where(second, rank, 0.0), axis=-1, keepdims=True)
    info_ref[...] = jnp.where(lane == 0, i1, jnp.where(lane == 1, i2, jnp.where(lane == 2, rank1,
                              jnp.where(lane == 3, rank2, 0.0))))
    xpk_ref[...] = _pack_bf16_pairs(x)


def _router(x, w_router):
    m, d = x.shape
    n_experts = w_router.shape[1]
    w_pad = jnp.pad(w_router, ((0, 0), (0, LANES - n_experts)))
    tm = min(512, m)
    row = lambda i: (i, 0)
    return pl.pallas_call(
        functools.partial(_router_kernel, n_experts=n_experts),
        out_shape=(jax.ShapeDtypeStruct((m, LANES), F32), jax.ShapeDtypeStruct((m, LANES), F32),
                   jax.ShapeDtypeStruct((m, d // 2), jnp.int32), jax.ShapeDtypeStruct((8, LANES), F32)),
        grid=(m // tm,),
        in_specs=[pl.BlockSpec((tm, d), row), pl.BlockSpec((d, LANES), lambda i: (0, 0))],
        out_specs=(pl.BlockSpec((tm, LANES), row), pl.BlockSpec((tm, LANES), row),
                   pl.BlockSpec((tm, d // 2), row), pl.BlockSpec((8, LANES), lambda i: (0, 0))),
        scratch_shapes=[pltpu.VMEM((8, LANES), F32)],
        compiler_params=_params("arbitrary"),
        name="router",
    )(x, w_pad)


def _moe_src_kernel(pos_ref, src_ref):
    def clear(i, c):
        src_ref[i] = 0
        return c

    lax.fori_loop(0, src_ref.shape[0], clear, 0, unroll=16)

    def place(t, c):
        src_ref[pos_ref[2 * t]] = t
        src_ref[pos_ref[2 * t + 1]] = t
        return c

    lax.fori_loop(0, pos_ref.shape[0] // 2, place, 0, unroll=8)


def _moe_src(pos, n_rows):
    smem = pl.BlockSpec(memory_space=pltpu.SMEM)
    return pl.pallas_call(
        _moe_src_kernel,
        out_shape=jax.ShapeDtypeStruct((n_rows,), jnp.int32),
        in_specs=[smem],
        out_specs=smem,
        name="moe_src",
    )(pos)


def _moe_dispatch_kernel(src_ref, xpk_ref, comb_ref, xs_ref, ws_ref):
    rows = xs_ref.shape[0]
    base = pl.program_id(0) * rows

    def copy(i, c):
        t = src_ref[base + i]
        xs_ref[pl.ds(i, 1), :] = xpk_ref[pl.ds(t, 1), :]
        ws_ref[pl.ds(i, 1), :] = comb_ref[pl.ds(t, 1), :]
        return c

    lax.fori_loop(0, rows, copy, 0, unroll=8)


def _moe_dispatch(src, xpk, comb, n_tiles):
    m, half = xpk.shape
    whole = lambda a: pl.BlockSpec(a.shape, lambda r, s: (0, 0), pipeline_mode=pl.Buffered(1))
    tile = lambda w: pl.BlockSpec((MOE_TILE, w), lambda r, s: (r, 0))
    return pl.pallas_call(
        _moe_dispatch_kernel,
        out_shape=(jax.ShapeDtypeStruct((n_tiles * MOE_TILE, half), jnp.int32),
                   jax.ShapeDtypeStruct((n_tiles * MOE_TILE, LANES), F32)),
        grid_spec=pltpu.PrefetchScalarGridSpec(
            num_scalar_prefetch=1, grid=(n_tiles,),
            in_specs=[whole(xpk), whole(comb)],
            out_specs=(tile(half), tile(LANES))),
        compiler_params=_params("arbitrary"),
        name="moe_dispatch",
    )(src, xpk, comb)


def _moe_up_kernel(te_ref, tv_ref, xs_ref, ws_ref, wg_ref, wu_ref, o_ref):
    r = pl.program_id(1)

    @pl.when(tv_ref[r] == 1)
    def _():
        x = _unpack_bf16_pairs(xs_ref[...])
        gate = _dot(x, wg_ref[0])
        h = gate * _sigmoid(gate) * _dot(x, wu_ref[0])
        ws = ws_ref[...]
        mine = _iota(ws.shape, 1) == te_ref[r]
        o_ref[...] = (h * jnp.sum(jnp.where(mine, ws, 0.0), axis=-1, keepdims=True)).astype(o_ref.dtype)

    @pl.when(tv_ref[r] == 0)
    def _():
        o_ref[...] = jnp.zeros_like(o_ref)


def _moe_up(tile_expert, tile_valid, xs, ws, w_gate_up, w_base):
    n_rows, half = xs.shape
    d = 2 * half
    d_ff = w_gate_up.shape[2] // 2
    n_split = 2
    tn = d_ff // n_split
    return pl.pallas_call(
        _moe_up_kernel,
        out_shape=jax.ShapeDtypeStruct((n_rows, d_ff), BF16),
        grid_spec=pltpu.PrefetchScalarGridSpec(
            num_scalar_prefetch=2, grid=(n_split, n_rows // MOE_TILE),
            in_specs=[pl.BlockSpec((MOE_TILE, half), lambda n, r, te, tv: (r, 0)),
                      pl.BlockSpec((MOE_TILE, LANES), lambda n, r, te, tv: (r, 0)),
                      pl.BlockSpec((1, d, tn), lambda n, r, te, tv: (w_base + te[r], 0, n)),
                      pl.BlockSpec((1, d, tn), lambda n, r, te, tv: (w_base + te[r], 0, n + n_split))],
            out_specs=pl.BlockSpec((MOE_TILE, tn), lambda n, r, te, tv: (r, n))),
        compiler_params=_params("arbitrary", "arbitrary"),
        name="moe_up",
    )(tile_expert, tile_valid, xs, ws, w_gate_up, w_gate_up)


def _moe_down_kernel(te_ref, tv_ref, h_ref, wd_ref, o_ref):
    r = pl.program_id(0)

    @pl.when(tv_ref[r] == 1)
    def _():
        o_ref[...] = _dot(h_ref[...], wd_ref[0]).astype(o_ref.dtype)

    @pl.when(tv_ref[r] == 0)
    def _():
        o_ref[...] = jnp.zeros_like(o_ref)


def _moe_down(tile_expert, tile_valid, h, w_down, w_base):
    n_rows, d_ff = h.shape
    d = w_down.shape[2]
    return pl.pallas_call(
        _moe_down_kernel,
        out_shape=jax.ShapeDtypeStruct((n_rows, d), BF16),
        grid_spec=pltpu.PrefetchScalarGridSpec(
            num_scalar_prefetch=2, grid=(n_rows // MOE_TILE,),
            in_specs=[pl.BlockSpec((MOE_TILE, d_ff), lambda r, te, tv: (r, 0)),
                      pl.BlockSpec((1, d_ff, d), lambda r, te, tv: (w_base + te[r], 0, 0))],
            out_specs=pl.BlockSpec((MOE_TILE, d), lambda r, te, tv: (r, 0))),
        compiler_params=_params("arbitrary"),
        name="moe_down",
    )(tile_expert, tile_valid, h, w_down)


def _moe_combine_kernel(win_ref, ok_ref, rs_ref, info_ref, x_ref, ln_ref, *refs, alpha, n_experts):
    n_win = 2 * n_experts
    windows, (o_ref, ob_ref) = refs[:n_win], refs[n_win:]
    i = pl.program_id(0)
    info = info_ref[...]
    i1, i2, rank1, rank2 = (info[:, c:c + 1] for c in range(4))
    pos1, pos2 = rank1, rank2
    for e in range(n_experts):
        start = rs_ref[e].astype(F32)
        pos1 = pos1 + jnp.where(i1 == e, start, 0.0)
        pos2 = pos2 + jnp.where(i2 == e, start, 0.0)
    rows = windows[0].shape[0]
    lane = _iota((1, rows), 1).astype(F32)
    acc = None
    for k in range(n_win):
        base = jnp.where(ok_ref[i * n_win + k] == 1, win_ref[i * n_win + k] * rows, -rows).astype(F32)
        pick = jnp.where(pos1 - base == lane, 1.0, jnp.where(pos2 - base == lane, 1.0, 0.0))
        part = _dot(pick, windows[k][...])
        acc = part if acc is None else acc + part
    y = _layer_norm(alpha * x_ref[...] + acc, ln_ref[0:1, :], ln_ref[1:2, :])
    o_ref[...] = y
    ob_ref[...] = y.astype(BF16)


def _moe_combine(win, ok, row_start, info, x, ln, out, alpha, n_experts):
    m, d = x.shape
    tt = COMBINE_TILE
    n_win = 2 * n_experts
    row = lambda i, *_: (i, 0)
    window = lambda k: pl.BlockSpec((tt, d), lambda i, w, o, s: (w[i * n_win + k], 0))
    return pl.pallas_call(
        functools.partial(_moe_combine_kernel, alpha=alpha, n_experts=n_experts),
        out_shape=(jax.ShapeDtypeStruct((m, d), F32), jax.ShapeDtypeStruct((m, d), BF16)),
        grid_spec=pltpu.PrefetchScalarGridSpec(
            num_scalar_prefetch=3, grid=(m // tt,),
            in_specs=[pl.BlockSpec((tt, LANES), row), pl.BlockSpec((tt, d), row),
                      pl.BlockSpec(ln.shape, lambda i, *_: (0, 0))] + [window(k) for k in range(n_win)],
            out_specs=(pl.BlockSpec((tt, d), row),) * 2),
        compiler_params=_params("arbitrary"),
        name="moe_combine",
    )(win, ok, row_start, info, x, ln, *([out] * n_win))


def _moe_swiglu(x, w_router, w_gate_up, w_down, w_base, ln, alpha):
    m, d = x.shape
    n_experts = w_router.shape[1]
    assert (TOP_K * m) % MOE_TILE == 0 and m % COMBINE_TILE == 0 and MOE_TILE % COMBINE_TILE == 0
    n_tiles = TOP_K * m // MOE_TILE + n_experts
    comb, info, xpk, counts = _router(x, w_router)

    i32 = jnp.int32
    experts = jnp.arange(n_experts, dtype=i32)
    counts = counts[0, :n_experts].astype(i32)
    tiles = (counts + MOE_TILE - 1) // MOE_TILE
    tile_end = jnp.cumsum(tiles)
    row_start = (tile_end - tiles) * MOE_TILE
    t = jnp.arange(n_tiles, dtype=i32)
    tile_expert = jnp.minimum(jnp.sum(t[:, None] >= tile_end[None, :], axis=1), n_experts - 1).astype(i32)
    tile_valid = (t < tile_end[-1]).astype(i32)
    i1, i2 = info[:, 0].astype(i32), info[:, 1].astype(i32)
    hit1, hit2 = i1[:, None] == experts[None, :], i2[:, None] == experts[None, :]
    pos1 = info[:, 2].astype(i32) + jnp.sum(jnp.where(hit1, row_start[None, :], 0), axis=1)
    pos2 = info[:, 3].astype(i32) + jnp.sum(jnp.where(hit2, row_start[None, :], 0), axis=1)
    per_tile = jnp.sum((hit1 | hit2).astype(i32).reshape(m // COMBINE_TILE, COMBINE_TILE, n_experts), axis=1)
    first = row_start[None, :] + jnp.cumsum(per_tile, axis=0) - per_tile
    win0 = first // COMBINE_TILE
    win1 = (first + jnp.maximum(per_tile, 1) - 1) // COMBINE_TILE
    ok0 = per_tile > 0
    ok1 = ok0 & (win1 != win0)
    win = jnp.stack([jnp.where(ok0, win0, 0), jnp.where(ok1, win1, 0)], axis=-1).reshape(-1).astype(i32)
    ok = jnp.stack([ok0, ok1], axis=-1).reshape(-1).astype(i32)

    src = _moe_src(jnp.stack([pos1, pos2], axis=-1).reshape(-1), n_tiles * MOE_TILE)
    xs, ws = _moe_dispatch(src, xpk, comb, n_tiles)
    h = _moe_up(tile_expert, tile_valid, xs, ws, w_gate_up, w_base)
    out = _moe_down(tile_expert, tile_valid, h, w_down, w_base)
    return _moe_combine(win, ok, row_start, info, x, ln, out, alpha, n_experts)


def _ffn_up_kernel(x_ref, wg_ref, wu_ref, o_ref):
    x = x_ref[...]
    gate = _dot(x, wg_ref[0])
    o_ref[0] = (gate * _sigmoid(gate) * _dot(x, wu_ref[0])).astype(o_ref.dtype)


def _ffn_up(xb, w_gate_up):
    m, d = xb.shape
    n_e = w_gate_up.shape[0]
    d_ff = w_gate_up.shape[2] // 2
    n_split = 2
    tn = d_ff // n_split
    tm = min(512, m)
    return pl.pallas_call(
        _ffn_up_kernel,
        out_shape=jax.ShapeDtypeStruct((n_e, m, d_ff), BF16),
        grid=(n_e, n_split, m // tm),
        in_specs=[pl.BlockSpec((tm, d), lambda e, n, i: (i, 0)),
                  pl.BlockSpec((1, d, tn), lambda e, n, i: (e, 0, n)),
                  pl.BlockSpec((1, d, tn), lambda e, n, i: (e, 0, n + n_split))],
        out_specs=pl.BlockSpec((1, tm, tn), lambda e, n, i: (e, i, n)),
        compiler_params=_params("arbitrary", "arbitrary", "arbitrary"),
        name="ffn_up",
    )(xb, w_gate_up, w_gate_up)


def _ffn_down_kernel(h_ref, wd_ref, x_ref, ln_ref, o_ref, ob_ref, acc_ref, *, alpha):
    e = pl.program_id(1)

    @pl.when(e == 0)
    def _():
        acc_ref[...] = jnp.zeros_like(acc_ref)

    acc_ref[...] += _dot(h_ref[0], wd_ref[0])

    @pl.when(e == pl.num_programs(1) - 1)
    def _():
        y = _layer_norm(alpha * x_ref[...] + acc_ref[...], ln_ref[0:1, :], ln_ref[1:2, :])
        o_ref[...] = y
        ob_ref[...] = y.astype(BF16)


def _ffn_down(h, w_down, x, ln, alpha):
    n_e, m, d_ff = h.shape
    d = x.shape[1]
    tm = min(512, m)
    row = lambda i, e: (i, 0)
    return pl.pallas_call(
        functools.partial(_ffn_down_kernel, alpha=alpha),
        out_shape=(jax.ShapeDtypeStruct((m, d), F32), jax.ShapeDtypeStruct((m, d), BF16)),
        grid=(m // tm, n_e),
        in_specs=[pl.BlockSpec((1, tm, d_ff), lambda i, e: (e, i, 0)),
                  pl.BlockSpec((1, d_ff, d), lambda i, e: (e, 0, 0)),
                  pl.BlockSpec((tm, d), row),
                  pl.BlockSpec(ln.shape, lambda i, e: (0, 0))],
        out_specs=(pl.BlockSpec((tm, d), row),) * 2,
        scratch_shapes=[pltpu.VMEM((tm, d), F32)],
        compiler_params=_params("arbitrary", "arbitrary"),
        name="ffn_down",
    )(h, w_down, x, ln)


def _pad_rows(w, rows_before, rows_total):
    return jnp.pad(w, ((rows_before, rows_total - rows_before - w.shape[0]), (0, 0)))


def kernel(x, mem, a_w_in_first, a_w_in_rest, a_mu_first, a_mu_rest, a_vec, a_w_up, a_a_up, a_g_up, a_v0, a_v_up, a_r_k, b_w_in, w_kv_shared, mem_kv, w_o, ln, ffn_gate_up, ffn_down, router, exp_gate_up, exp_down):
    batch, seq, d_model = x.shape
    mem_tokens = mem.shape[1]
    depth = w_o.shape[0]
    n_a = a_vec.shape[0]
    alpha = (2 * depth) ** 0.25
    m = batch * seq
    rkv_w = 3 * MIX_WIDTH
    lora_w = DECAY_LORA + ICLR_LORA + GATE_LORA
    assert DECAY_LORA + ICLR_LORA == LANES and lora_w == 2 * LANES and rkv_w % lora_w == 0

    xf = x.reshape(m, d_model)
    xb = xf.astype(BF16)
    memb = mem.reshape(batch * mem_tokens, d_model).astype(BF16)
    hh = jnp.arange(MIX_WIDTH) // HEAD_DIM
    block_ones = (hh[:, None] == hh[None, :]).astype(BF16)

    v_first = None
    kv = None
    for l in range(depth):
        mkv = _matmul(memb, mem_kv[l].astype(BF16), tn=2 * MEM_WIDTH)
        if l < n_a:
            w_in = a_w_in_first if l == 0 else a_w_in_rest[l - 1]
            mu = a_mu_first if l == 0 else a_mu_rest[l - 1]
            has_vres = l > 0
            n_cols = w_in.shape[1] - MEM_WIDTH
            gate_lo = n_cols - GATE_LORA
            order = jnp.concatenate([jnp.arange(rkv_w + DECAY_LORA + ICLR_LORA), jnp.arange(gate_lo, n_cols),
                                     jnp.arange(n_cols, n_cols + MEM_WIDTH)])
            proj = _matmul(xb, w_in[:, order].astype(BF16), tn=(rkv_w + lora_w + MEM_WIDTH) // 2)
            mu_rkv = mu[:rkv_w].reshape(1, rkv_w)
            mu_lora = jnp.concatenate([mu[rkv_w:rkv_w + LANES], mu[gate_lo:n_cols]]).reshape(1, lora_w)
            ww = _pad_rows(a_w_up[l], 0, LANES).astype(BF16)
            wa = _pad_rows(a_a_up[l], DECAY_LORA, LANES).astype(BF16)
            wg = a_g_up[l].astype(BF16)
            vec = jnp.concatenate([a_vec[l, :4], a_v0[l - 1][None] if has_vres else jnp.zeros((1, MIX_WIDTH), F32),
                                   jnp.zeros((3, MIX_WIDTH), F32)])
            if has_vres:
                vd_lo = rkv_w + LANES
                w_vd = jnp.pad(w_in[:, vd_lo:vd_lo + VALUE_LORA], ((0, 0), (0, LANES - VALUE_LORA)))
                vd = _matmul(xb, w_vd.astype(BF16), tn=LANES)
                mu_vd = jnp.pad(mu[vd_lo:vd_lo + VALUE_LORA], (0, LANES - VALUE_LORA)).reshape(1, LANES)
                wv = _pad_rows(a_v_up[l - 1], 0, LANES).astype(BF16)
                ops = _rwkv_prep(proj, vd, v_first, mu_rkv, mu_lora, mu_vd, vec, ww, wa, wg, wv, block_ones,
                                 batch, seq)
            else:
                ops = _rwkv_prep(proj, None, None, mu_rkv, mu_lora, None, vec, ww, wa, wg, None, block_ones,
                                 batch, seq)
                v_first = ops[3]
            scan_vec = jnp.concatenate([a_vec[l, 4:6], a_r_k[l].reshape(1, MIX_WIDTH),
                                        jnp.zeros((5, MIX_WIDTH), F32)])
            tok = _wkv_scan(*ops, scan_vec, batch, seq)
            q_col_block = (rkv_w + lora_w) // MEM_WIDTH
        else:
            proj = _matmul(xb, b_w_in[l - n_a].astype(BF16), tn=(MIX_WIDTH + MEM_WIDTH) // 2)
            tok = _stick_attention(proj, kv, batch, seq)
            q_col_block = MIX_WIDTH // MEM_WIDTH
        mo = _mem_attention(proj, q_col_block, mkv, batch, seq, mem_tokens)
        xf, xb = _mix_ln(tok, mo, xf, w_o[l].astype(BF16), ln[l, 0:2], alpha)
        if l % 2 == 0:
            h = _ffn_up(xb, ffn_gate_up[l // 2][None].astype(BF16))
            xf, xb = _ffn_down(h, ffn_down[l // 2][None].astype(BF16), xf, ln[l, 2:4], alpha)
        else:
            n_e = router.shape[2]
            xf, xb = _moe_swiglu(xf, router[l // 2], exp_gate_up.reshape((-1,) + exp_gate_up.shape[2:]),
                                 exp_down.reshape((-1,) + exp_down.shape[2:]), (l // 2) * n_e, ln[l, 2:4], alpha)
        if l == n_a - 1:
            kv = _matmul(xb, w_kv_shared.astype(BF16), tn=MIX_WIDTH)
    return xf.reshape(batch, seq, d_model)
```
